```python
import math
import jax, jax.numpy as jnp
from jax import lax
import numpy as np

D_MODEL = 1024
BATCH = 16
SEQ = 4096
DEPTH = 2
DEC_BATCH = 16
DEC_SEQ = 2048
PAST_LEN = 128

HEAD_DIM = 64
N_HEADS_ATTN = 6
ATTN_WIDTH = N_HEADS_ATTN * HEAD_DIM
CONV_WIDTH = 256
N_HEADS_RWKV = 6
RWKV_WIDTH = N_HEADS_RWKV * HEAD_DIM
MIX_WIDTH = ATTN_WIDTH + CONV_WIDTH + RWKV_WIDTH
DECAY_RANK = 32
ICLR_RANK = 32
GATE_RANK = 64
RWKV_IN = 3 * RWKV_WIDTH + DECAY_RANK + ICLR_RANK + GATE_RANK
IN_WIDTH = 3 * ATTN_WIDTH + 3 * CONV_WIDTH + RWKV_IN
D_FF = 4 * D_MODEL
DILATED_PATTERNS = ((128, 1), (512, 4), (2048, 16))
QUERY_BLOCK = 64
N_BUCKETS = 32
BUCKET_MAX_DIST = 1024
RMS_EPS = 1e-6
LNX_EPS = 64e-5

kernel_name = "hybrid_dilated_conv_rwkv7_encoder"


def rms_norm(x, g):
    xf = x.astype(jnp.float32)
    y = xf * lax.rsqrt(jnp.mean(xf * xf, axis=-1, keepdims=True) + RMS_EPS)
    return (y * g.astype(jnp.float32)).astype(x.dtype)


def t5_bucket(rel):
    half = N_BUCKETS // 2
    max_exact = half // 2
    ret = np.where(rel > 0, half, 0)
    n = np.abs(rel)
    large = max_exact + (np.log(np.maximum(n, 1) / max_exact)
                         / np.log(BUCKET_MAX_DIST / max_exact) * (half - max_exact)).astype(np.int32)
    large = np.minimum(large, half - 1)
    return (ret + np.where(n < max_exact, n, large)).astype(np.int32)


def dilated_branch(q, k, v, rel_bias, window, dil):
    B, S, H, Dh = q.shape
    R = window // (2 * dil)
    L = S // dil
    qb = math.gcd(L, QUERY_BLOCK)
    nb = L // qb
    W = qb + 2 * R
    qc = q.reshape(B, nb, qb, dil, H, Dh)
    kp = jnp.pad(k.reshape(B, L, dil, H, Dh), ((0, 0), (R, R), (0, 0), (0, 0), (0, 0)))
    vp = jnp.pad(v.reshape(B, L, dil, H, Dh), ((0, 0), (R, R), (0, 0), (0, 0), (0, 0)))
    win = np.arange(nb)[:, None] * qb + np.arange(W)[None, :]
    kw = kp[:, win]
    vw = vp[:, win]
    logits = jnp.einsum('bnqchd,bnwchd->bnchqw', qc, kw).astype(jnp.float32) * (Dh ** -0.5)
    rel = np.arange(W)[None, :] - R - np.arange(qb)[:, None]
    bias = jnp.transpose(rel_bias[t5_bucket(rel * dil)], (2, 0, 1)).astype(jnp.float32)
    gidx = win - R
    valid = (np.abs(rel)[None] <= R) & (gidx[:, None, :] >= 0) & (gidx[:, None, :] < L)
    logits = jnp.where(valid[None, :, None, None], logits + bias, -jnp.inf)
    m = jnp.max(logits, axis=-1)
    p = jnp.exp(logits - m[..., None])
    s = jnp.sum(p, axis=-1)
    o = jnp.einsum('bnchqw,bnwchd->bnqchd', p, vw.astype(jnp.float32))
    m = jnp.transpose(m, (0, 1, 4, 2, 3)).reshape(B, S, H)
    s = jnp.transpose(s, (0, 1, 4, 2, 3)).reshape(B, S, H)
    o = o.reshape(B, S, H, Dh) / s[..., None]
    return m, s, o


def dilated_attention(q, k, v, rel_bias):
    outs = [dilated_branch(q, k, v, rel_bias, wnd, d) for (wnd, d) in DILATED_PATTERNS]
    m_all = jnp.stack([o[0] for o in outs])
    s_all = jnp.stack([o[1] for o in outs])
    o_all = jnp.stack([o[2] for o in outs])
    den = s_all * jnp.exp(m_all - jnp.max(m_all, axis=0))
    out = jnp.sum(den[..., None] * o_all, axis=0) / jnp.sum(den, axis=0)[..., None]
    return out.astype(q.dtype)


def short_conv_mixer(b_gate, c_gate, hv, conv_w):
    u = c_gate * hv
    up = jnp.pad(u, ((0, 0), (1, 1), (0, 0)))
    c = up[:, :-2] * conv_w[0] + up[:, 1:-1] * conv_w[1] + up[:, 2:] * conv_w[2]
    return b_gate * c


def centred_shift(z):
    zp = jnp.pad(z, ((0, 0), (1, 1), (0, 0)))
    return 0.5 * (zp[:, :-2] + zp[:, 2:])


def _wkv_step(S, inp):
    r, w, k, v, a_, b_ = inp
    sa = jnp.einsum('dbhij,dbhj->dbhi', S, a_)
    S = S * w[..., None, :] + sa[..., None] * b_[..., None, :] + v[..., None] * k[..., None, :]
    y = jnp.einsum('dbhij,dbhj->dbhi', S, r)
    return S, y


def rwkv7_mixer(zc, mu, w0, w_up, a0, a_up, g_up, k_k, k_a, r_k, lnx_w, lnx_b):
    dtype = zc.dtype
    B, T, _ = zc.shape
    H, N, C = N_HEADS_RWKV, HEAD_DIM, RWKV_WIDTH
    zc = zc.astype(jnp.float32)
    zc = zc + (centred_shift(zc) - zc) * mu
    r, k, v, wd, ad, gd = jnp.split(zc, [C, 2 * C, 3 * C, 3 * C + DECAY_RANK,
                                         3 * C + DECAY_RANK + ICLR_RANK], axis=-1)
    w_log = -jax.nn.softplus(-(w0[:, None, None, :]
                               + jnp.einsum('btr,drc->dbtc', jnp.tanh(wd), w_up))) - 0.5
    decay = jnp.exp(-jnp.exp(w_log))
    a = jax.nn.sigmoid(a0[:, None, None, :] + jnp.einsum('btr,drc->dbtc', ad, a_up))
    g = jax.nn.sigmoid(gd) @ g_up
    hd = lambda t: t.reshape(t.shape[:-1] + (H, N))
    kk = hd(k * k_k)
    kk = kk / jnp.maximum(jnp.sqrt(jnp.sum(kk * kk, axis=-1, keepdims=True)), 1e-12)
    k_dir = hd(k[None] * (1.0 + (a - 1.0) * k_a))
    b_dir = kk[None] * hd(a)
    r_h, v_h = hd(r), hd(v)

    def time_major(t_dir):
        t_dir = jnp.stack([t_dir[0], jnp.flip(t_dir[1], axis=1)])
        return jnp.transpose(t_dir, (2, 0, 1, 3, 4))

    both = lambda t: jnp.broadcast_to(t[None], (2,) + t.shape)
    xs = (time_major(both(r_h)), time_major(hd(decay)), time_major(k_dir),
          time_major(both(v_h)), time_major(both(-kk)), time_major(b_dir))
    S0 = jnp.zeros((2, B, H, N, N), jnp.float32)
    _, y = lax.scan(_wkv_step, S0, xs)
    y = y[:, 0] + jnp.flip(y[:, 1], axis=0)
    y = jnp.transpose(y, (1, 0, 2, 3))
    mean = jnp.mean(y, axis=-1, keepdims=True)
    var = jnp.mean(jnp.square(y - mean), axis=-1, keepdims=True)
    yn = (y - mean) * lax.rsqrt(var + LNX_EPS) * hd(lnx_w) + hd(lnx_b)
    bonus = jnp.sum(r_h[None] * k_dir * r_k, axis=(0, -1))[..., None] * v_h
    out = (yn + bonus).reshape(B, T, C) * g
    return out.astype(dtype)


def _layer(x, rel_bias, g_mix_pre, g_mix_post, g_ffn_pre, g_ffn_post, w_in, w_out,
           attn_out_g, conv_w, conv_out_g, rwkv_mu, decay_w0, decay_up, iclr_a0, iclr_up,
           gate_up, k_k, k_a, r_k, lnx_w, lnx_b, ffn_w1, ffn_w2):
    B, T, _ = x.shape
    h = rms_norm(x, g_mix_pre)
    z = h @ w_in
    o1 = 3 * ATTN_WIDTH
    o2 = o1 + 3 * CONV_WIDTH
    q, k, v = [t.reshape(B, T, N_HEADS_ATTN, HEAD_DIM) for t in jnp.split(z[..., :o1], 3, axis=-1)]
    b_gate, c_gate, hv = jnp.split(z[..., o1:o2], 3, axis=-1)
    zc = z[..., o2:]
    ya = rms_norm(dilated_attention(q, k, v, rel_bias).reshape(B, T, ATTN_WIDTH), attn_out_g)
    yb = rms_norm(short_conv_mixer(b_gate, c_gate, hv, conv_w), conv_out_g)
    yc = rwkv7_mixer(zc, rwkv_mu, decay_w0, decay_up, iclr_a0, iclr_up, gate_up,
                     k_k, k_a, r_k, lnx_w, lnx_b)
    mix = jnp.concatenate([ya, yb, yc], axis=-1) @ w_out
    x = x + rms_norm(mix, g_mix_post)
    hf = rms_norm(x, g_ffn_pre)
    f = jnp.square(jax.nn.relu(hf @ ffn_w1)) @ ffn_w2
    return x + rms_norm(f, g_ffn_post)


def _trunk(x, rel_bias, layer_params):
    for l in range(DEPTH):
        x = _layer(x, rel_bias, *[p[l] for p in layer_params])
    return x


def setup_inputs(seed: int = 0) -> dict:
    key = jax.random.key(seed)
    ks = jax.random.split(key, 32)
    n = lambda i, shape: jax.random.normal(ks[i], shape, jnp.float32)
    L = DEPTH
    return {
        "x_prompt": n(0, (BATCH, SEQ, D_MODEL)),
        "x_sample": n(1, (DEC_BATCH, DEC_SEQ, D_MODEL)),
        "rel_bias": 0.2 * n(2, (N_BUCKETS, N_HEADS_ATTN)),
        "norm_mix_pre": 1.0 + 0.05 * n(3, (L, D_MODEL)),
        "norm_mix_post": 1.0 + 0.05 * n(4, (L, D_MODEL)),
        "norm_ffn_pre": 1.0 + 0.05 * n(5, (L, D_MODEL)),
        "norm_ffn_post": 1.0 + 0.05 * n(6, (L, D_MODEL)),
        "w_in": n(7, (L, D_MODEL, IN_WIDTH)) * D_MODEL ** -0.5,
        "w_out": n(8, (L, MIX_WIDTH, D_MODEL)) * MIX_WIDTH ** -0.5,
        "attn_out_g": 1.0 + 0.05 * n(9, (L, ATTN_WIDTH)),
        "conv_w": n(10, (L, 3, CONV_WIDTH)) * 3 ** -0.5,
        "conv_out_g": 1.0 + 0.05 * n(11, (L, CONV_WIDTH)),
        "rwkv_mu": jax.random.uniform(ks[12], (L, RWKV_IN), jnp.float32),
        "decay_w0": jnp.linspace(-6.0, -0.5, RWKV_WIDTH, dtype=jnp.float32)[None, None, :]
                    + 0.3 * n(13, (L, 2, RWKV_WIDTH)),
        "decay_up": 0.1 * n(14, (L, 2, DECAY_RANK, RWKV_WIDTH)),
        "iclr_a0": 0.5 * n(15, (L, 2, RWKV_WIDTH)),
        "iclr_up": 0.5 * n(16, (L, 2, ICLR_RANK, RWKV_WIDTH)) * ICLR_RANK ** -0.5,
        "gate_up": n(17, (L, GATE_RANK, RWKV_WIDTH)) * GATE_RANK ** -0.5,
        "k_k": 0.85 + 0.1 * n(18, (L, RWKV_WIDTH)),
        "k_a": 1.0 + 0.1 * n(19, (L, RWKV_WIDTH)),
        "r_k": 0.1 * n(20, (L, N_HEADS_RWKV, HEAD_DIM)),
        "lnx_w": 1.0 + 0.05 * n(21, (L, RWKV_WIDTH)),
        "lnx_b": 0.02 * n(22, (L, RWKV_WIDTH)),
        "ffn_w1": n(23, (L, D_MODEL, D_FF)) * D_MODEL ** -0.5,
        "ffn_w2": n(24, (L, D_FF, D_MODEL)) * D_FF ** -0.5,
    }


def reference(x_prompt, x_sample, rel_bias, norm_mix_pre, norm_mix_post, norm_ffn_pre,
              norm_ffn_post, w_in, w_out, attn_out_g, conv_w, conv_out_g, rwkv_mu, decay_w0,
              decay_up, iclr_a0, iclr_up, gate_up, k_k, k_a, r_k, lnx_w, lnx_b, ffn_w1, ffn_w2):
    layer_params = (norm_mix_pre, norm_mix_post, norm_ffn_pre, norm_ffn_post, w_in, w_out,
                    attn_out_g, conv_w, conv_out_g, rwkv_mu, decay_w0, decay_up, iclr_a0,
                    iclr_up, gate_up, k_k, k_a, r_k, lnx_w, lnx_b, ffn_w1, ffn_w2)
    y_prompt = _trunk(x_prompt, rel_bias, layer_params)
    y_sample = _trunk(x_sample, rel_bias, layer_params)
    return (y_prompt, y_sample)
```

```python
import functools
import math

import numpy as np
import jax
import jax.numpy as jnp
from jax import lax
from jax.experimental import pallas as pl
from jax.experimental.pallas import tpu as pltpu

F32 = jnp.float32
BF16 = jnp.bfloat16

D_MODEL = 1024
HEAD_DIM = 64
N_HEADS = 6
HEADS_W = N_HEADS * HEAD_DIM
CONV_W = 256
QKV_W = 3 * HEADS_W
CONVIN_W = 3 * CONV_W
LORA_W = 128
DECAY_RANK = 32
ICLR_RANK = 32
RWKV_IN = 3 * HEADS_W + LORA_W
IN_WIDTH = QKV_W + CONVIN_W + RWKV_IN
D_FF = 4 * D_MODEL
DILATIONS = (1, 4, 16)
KEYS_PER_SIDE = 64
N_BUCKETS = 32
BUCKET_MAX_DIST = 1024
RMS_EPS = 1e-6
LNX_EPS = 64e-5
CHUNK = 64
LANES = 128
N_PAIRS = HEADS_W // LANES
HALO_ROWS = 8
VMEM_LIMIT = 56 * 1024 * 1024


def _rms(x, g, eps=RMS_EPS):
    return x * lax.rsqrt(jnp.mean(x * x, axis=-1, keepdims=True) + eps) * g


def _sigmoid(x):
    return 1.0 / (1.0 + jnp.exp(-x))


def _dot(a, b):
    return jnp.dot(a, b, preferred_element_type=F32)


def _dot_tb(a, b):
    return lax.dot_general(a, b, (((1,), (1,)), ((), ())), preferred_element_type=F32)


def _dot_ta(a, b):
    return lax.dot_general(a, b, (((0,), (0,)), ((), ())), preferred_element_type=F32)


def _split2(x):
    hi = x.astype(BF16)
    lo = (x - hi.astype(F32)).astype(BF16)
    return hi, lo


def _split3(x):
    hi = x.astype(BF16)
    r1 = x - hi.astype(F32)
    mid = r1.astype(BF16)
    lo = (r1 - mid.astype(F32)).astype(BF16)
    return hi, mid, lo


def _segsum(x, seg_ones):
    hi, lo = _split2(x)
    return _dot(hi, seg_ones) + _dot(lo, seg_ones)


def _inproj_kernel(x_ref, g_ref, w_ref, zq_ref, zb_ref, zc_ref):
    h = _rms(x_ref[...], g_ref[...]).astype(BF16)
    zq_ref[...] = _dot(h, w_ref[:, 0:QKV_W])
    zb_ref[...] = _dot(h, w_ref[:, QKV_W:QKV_W + CONVIN_W])
    zc_ref[...] = _dot(h, w_ref[:, QKV_W + CONVIN_W:IN_WIDTH])


def _inproj(x2, g, w_bf16, tm):
    n = x2.shape[0]
    row = lambda i: (i, 0)
    const = lambda i: (0, 0)
    return pl.pallas_call(
        _inproj_kernel,
        grid=(n // tm,),
        in_specs=[pl.BlockSpec((tm, D_MODEL), row),
                  pl.BlockSpec((1, D_MODEL), const),
                  pl.BlockSpec((D_MODEL, IN_WIDTH), const)],
        out_specs=[pl.BlockSpec((tm, QKV_W), row),
                   pl.BlockSpec((tm, CONVIN_W), row),
                   pl.BlockSpec((tm, RWKV_IN), row)],
        out_shape=[jax.ShapeDtypeStruct((n, QKV_W), F32),
                   jax.ShapeDtypeStruct((n, CONVIN_W), F32),
                   jax.ShapeDtypeStruct((n, RWKV_IN), F32)],
        compiler_params=pltpu.CompilerParams(
            dimension_semantics=("parallel",), vmem_limit_bytes=VMEM_LIMIT),
        name="inproj",
    )(x2, g, w_bf16)


def _t5_bucket(rel):
    half = N_BUCKETS // 2
    max_exact = half // 2
    ret = np.where(rel > 0, half, 0)
    n = np.abs(rel)
    large = max_exact + (np.log(np.maximum(n, 1) / max_exact)
                         / np.log(BUCKET_MAX_DIST / max_exact) * (half - max_exact)).astype(np.int32)
    large = np.minimum(large, half - 1)
    return (ret + np.where(n < max_exact, n, large)).astype(np.int32)


def _attn_cfg(t, dil):
    cls_len = t // dil
    bq = min(128, cls_len)
    wk = min(bq + 2 * KEYS_PER_SIDE, cls_len)
    return cls_len, bq, wk, cls_len // bq


def _attn_bias(rel_bias, t, dil):
    _, bq, wk, _ = _attn_cfg(t, dil)
    out = []
    for shift in (0, KEYS_PER_SIDE, wk - bq):
        rel = np.arange(wk)[None, :] - np.arange(bq)[:, None] - shift
        valid = np.abs(rel) <= KEYS_PER_SIDE
        bias = jnp.transpose(rel_bias[_t5_bucket(rel * dil)], (2, 0, 1)).astype(F32)
        out.append(jnp.where(valid[None], bias, -jnp.inf))
    return jnp.stack(out)


def _attn_kernel(q_ref, k_ref, v_ref, b0_ref, b1_ref, b2_ref, o_ref, m_ref, s_ref, *, t):
    lane = lax.broadcasted_iota(jnp.int32, (1, LANES), 1)
    head0 = lane < HEAD_DIM
    bias_refs = (b0_ref, b1_ref, b2_ref)

    for di, dil in enumerate(DILATIONS):
        cls_len, bq, wk, nb = _attn_cfg(t, dil)
        bias_ref = bias_refs[di]

        def rows(start, size, dil=dil):
            if dil == 1:
                return pl.ds(start, size)
            return pl.ds(start, size, stride=dil)

        def do_block(c, blk, placement, di=di, dil=dil, bq=bq, wk=wk, cls_len=cls_len,
                     bias_ref=bias_ref, rows=rows):
            m0 = blk * bq
            if placement == 0:
                ws = 0
            elif placement == 1:
                ws = m0 - KEYS_PER_SIDE
            else:
                ws = cls_len - wk
            qrows = rows(c + dil * m0, bq)
            krows = rows(c + dil * ws, wk)
            q = q_ref[qrows, :] * (HEAD_DIM ** -0.5)
            kw = k_ref[krows, :].astype(BF16)
            vw = v_ref[krows, :].astype(BF16)
            stats = []
            for h in range(2):
                hm = head0 if h == 0 else jnp.logical_not(head0)
                qh = jnp.where(hm, q, 0.0).astype(BF16)
                logits = _dot_tb(qh, kw) + bias_ref[placement, h]
                mh = jnp.max(logits, axis=-1, keepdims=True)
                p = jnp.exp(logits - mh)
                sh = jnp.sum(p, axis=-1, keepdims=True)
                oh = _dot(p.astype(BF16), vw)
                stats.append((mh, sh, oh))
            m_blk = jnp.where(head0, stats[0][0], stats[1][0])
            s_blk = jnp.where(head0, stats[0][1], stats[1][1])
            o_blk = jnp.where(head0, stats[0][2], stats[1][2])
            if di == 0:
                m_ref[qrows, :] = m_blk
                s_ref[qrows, :] = s_blk
                o_ref[qrows, :] = o_blk
            else:
                m_old = m_ref[qrows, :]
                m_new = jnp.maximum(m_old, m_blk)
                a_old = jnp.exp(m_old - m_new)
                a_blk = jnp.exp(m_blk - m_new)
                m_ref[qrows, :] = m_new
                s_ref[qrows, :] = s_ref[qrows, :] * a_old + s_blk * a_blk
                o_ref[qrows, :] = o_ref[qrows, :] * a_old + o_blk * a_blk

        def class_body(c, carry, nb=nb, do_block=do_block):
            do_block(c, 0, 0)
            if nb > 2:
                def mid(blk, cc):
                    do_block(c, blk, 1)
                    return cc
                lax.fori_loop(1, nb - 1, mid, 0)
            if nb > 1:
                do_block(c, nb - 1, 2)
            return carry

        if dil == 1:
            class_body(0, 0)
        else:
            lax.fori_loop(0, dil, class_body, 0)

    rb = min(256, t)

    def norm_body(i, carry):
        sl = pl.ds(pl.multiple_of(i * rb, rb), rb)
        o_ref[sl, :] = o_ref[sl, :] / s_ref[sl, :]
        return carry

    lax.fori_loop(0, t // rb, norm_body, 0)


def _attention(zq, biases, b, t):
    zq3 = zq.reshape(b, t, QKV_W)
    in_specs = [pl.BlockSpec((None, t, LANES), lambda i, hp: (i, 0, hp)),
                pl.BlockSpec((None, t, LANES), lambda i, hp: (i, 0, N_PAIRS + hp)),
                pl.BlockSpec((None, t, LANES), lambda i, hp: (i, 0, 2 * N_PAIRS + hp))]
    for bias in biases:
        in_specs.append(pl.BlockSpec((3, 2) + bias.shape[2:], lambda i, hp: (0, hp, 0, 0)))
    out = pl.pallas_call(
        functools.partial(_attn_kernel, t=t),
        grid=(b, N_PAIRS),
        in_specs=in_specs,
        out_specs=pl.BlockSpec((None, t, LANES), lambda i, hp: (i, 0, hp)),
        out_shape=jax.ShapeDtypeStruct((b, t, HEADS_W), F32),
        scratch_shapes=[pltpu.VMEM((t, LANES), F32), pltpu.VMEM((t, LANES), F32)],
        compiler_params=pltpu.CompilerParams(
            dimension_semantics=("parallel", "parallel"), vmem_limit_bytes=VMEM_LIMIT),
        name="dilated_attention",
    )(zq3, zq3, zq3, *biases)
    return out.reshape(b * t, HEADS_W)


def _neighbours(main, prev_row, next_row):
    tt = main.shape[0]
    row = lax.broadcasted_iota(jnp.int32, main.shape, 0)
    prev = jnp.where(row == 0, prev_row, pltpu.roll(main, 1, 0))
    nxt = jnp.where(row == tt - 1, next_row, pltpu.roll(main, tt - 1, 0))
    return prev, nxt


def _halo_rows(hp_ref, hn_ref, tau, nt):
    prev_row = jnp.where(tau > 0, hp_ref[HALO_ROWS - 1:HALO_ROWS, :], 0.0)
    next_row = jnp.where(tau < nt - 1, hn_ref[0:1, :], 0.0)
    return prev_row, next_row


def _token_shift(zc_ref, hp_ref, hn_ref, mu_ref, tau, nt):
    zc = zc_ref[...]
    prev_row, next_row = _halo_rows(hp_ref, hn_ref, tau, nt)
    prev, nxt = _neighbours(zc, prev_row, next_row)
    return zc + (0.5 * (prev + nxt) - zc) * mu_ref[...]


def _halo_specs(width, tt, t, tile_of_step):
    per = tt // HALO_ROWS
    last = t // HALO_ROWS - 1
    main = pl.BlockSpec((None, tt, width), lambda i, s: (i, tile_of_step(s), 0))
    prev = pl.BlockSpec((None, HALO_ROWS, width),
                        lambda i, s: (i, jnp.maximum(tile_of_step(s) * per - 1, 0), 0))
    nxt = pl.BlockSpec((None, HALO_ROWS, width),
                       lambda i, s: (i, jnp.minimum((tile_of_step(s) + 1) * per, last), 0))
    return [main, prev, nxt]


def _scan_kernel(zc_ref, hp_ref, hn_ref, mu_ref, w0_ref, wdec_ref, a0_ref, wa_ref,
                 kk_ref, ka_ref, seg_ref, y_ref, state_ref, *, tt, nt, rev):
    step = pl.program_id(1)
    tau = (nt - 1 - step) if rev else step

    @pl.when(step == 0)
    def _():
        state_ref[...] = jnp.zeros_like(state_ref)

    zs = _token_shift(zc_ref, hp_ref, hn_ref, mu_ref, tau, nt)
    r = zs[:, 0:HEADS_W]
    k = zs[:, HEADS_W:2 * HEADS_W]
    v = zs[:, 2 * HEADS_W:3 * HEADS_W]
    lora = zs[:, 3 * HEADS_W:RWKV_IN]

    xdec = w0_ref[...] + _dot(jnp.tanh(lora).astype(BF16), wdec_ref[...])
    lw = (-math.exp(-0.5)) * _sigmoid(xdec)
    asig = _sigmoid(a0_ref[...] + _dot(lora.astype(BF16), wa_ref[...]))
    kk = k * kk_ref[...]
    kk = kk / jnp.maximum(jnp.sqrt(_segsum(kk * kk, seg_ref[...])), 1e-12)
    kdir = k * (1.0 + (asig - 1.0) * ka_ref[...])
    bvec = kk * asig

    row = lax.broadcasted_iota(jnp.int32, (tt, tt), 0)
    col = lax.broadcasted_iota(jnp.int32, (tt, tt), 1)
    same = (row // CHUNK) == (col // CHUNK)
    if rev:
        incl = jnp.logical_and(same, col >= row)
        strict = jnp.logical_and(same, col > row)
    else:
        incl = jnp.logical_and(same, col <= row)
        strict = jnp.logical_and(same, col < row)
    tri = jnp.where(incl, 1.0, 0.0).astype(BF16)
    blk = jnp.where(same, 1.0, 0.0).astype(BF16)
    eye_tt = jnp.where(row == col, 1.0, 0.0)

    lw3 = _split3(lw)
    cum = _dot(tri, lw3[0]) + _dot(tri, lw3[1]) + _dot(tri, lw3[2])
    tot = _dot(blk, lw3[0]) + _dot(blk, lw3[1]) + _dot(blk, lw3[2])
    e_neg = jnp.exp(-cum)
    e_end = jnp.exp(tot - cum)
    r_t = r * jnp.exp(cum)
    k_t = kdir * e_neg
    b_t = bvec * e_neg
    a_t = -kk * jnp.exp(cum - lw)
    b_e = bvec * e_end
    k_e = kdir * e_end
    p_end = jnp.exp(tot)

    lane = lax.broadcasted_iota(jnp.int32, (1, LANES), 1)
    head0 = lane < HEAD_DIM
    prow = lax.broadcasted_iota(jnp.int32, (LANES, LANES), 0)
    pcol = lax.broadcasted_iota(jnp.int32, (LANES, LANES), 1)
    pair_diag = (prow // HEAD_DIM) == (pcol // HEAD_DIM)

    n_chunks = tt // CHUNK
    order = range(n_chunks - 1, -1, -1) if rev else range(n_chunks)

    for p in range(N_PAIRS):
        lp = slice(p * LANES, (p + 1) * LANES)
        r_p, k_p, b_p, a_p = r_t[:, lp], k_t[:, lp], b_t[:, lp], a_t[:, lp]
        v_p = v[:, lp]
        v_pb = v_p.astype(BF16)
        rhs = jnp.concatenate([b_p, k_p], axis=0).astype(BF16)
        q_pair = ta_pair = tv_pair = y_pair = None
        for h in range(2):
            hm = head0 if h == 0 else jnp.logical_not(head0)
            a_h = jnp.where(hm, a_p, 0.0)
            r_h = jnp.where(hm, r_p, 0.0)
            lhs = jnp.concatenate([a_h, r_h], axis=0).astype(BF16)
            s_all = _dot_tb(lhs, rhs)
            a_ab = jnp.where(strict, s_all[:tt, :tt], 0.0)
            a_ak = jnp.where(strict, s_all[:tt, tt:], 0.0)
            a_rb = jnp.where(incl, s_all[tt:, :tt], 0.0).astype(BF16)
            a_rk = jnp.where(incl, s_all[tt:, tt:], 0.0).astype(BF16)
            x = a_ab
            tinv = eye_tt + x
            for _ in range(5):
                xb = x.astype(BF16)
                x = _dot(xb, xb)
                tinv = tinv + _dot(tinv.astype(BF16), x.astype(BF16))
            tb = tinv.astype(BF16)
            ta_h = _dot(tb, a_h.astype(BF16))
            tv_h = _dot(tb, _dot(a_ak.astype(BF16), v_pb).astype(BF16))
            q_h = r_h + _dot(a_rb, ta_h.astype(BF16))
            y_h = _dot(a_rb, tv_h.astype(BF16)) + _dot(a_rk, v_pb)
            if h == 0:
                q_pair, ta_pair, tv_pair, y_pair = q_h, ta_h, tv_h, y_h
            else:
                q_pair = q_pair + q_h
                ta_pair = ta_pair + ta_h
                tv_pair = jnp.where(head0, tv_pair, tv_h)
                y_pair = jnp.where(head0, y_pair, y_h)

        be_p, ke_p, pe_p = b_e[:, lp], k_e[:, lp], p_end[:, lp]
        state = state_ref[p]
        for ci in order:
            sl = slice(ci * CHUNK, (ci + 1) * CHUNK)
            xt = _dot_ta(ta_pair[sl].astype(BF16), be_p[sl].astype(BF16))
            xt = jnp.where(pair_diag, xt, 0.0)
            ht = _dot_ta(jnp.concatenate([tv_pair[sl], v_p[sl]], axis=0).astype(BF16),
                         jnp.concatenate([be_p[sl], ke_p[sl]], axis=0).astype(BF16))
            ht = jnp.where(pair_diag, ht, 0.0)
            sb = state.astype(BF16)
            y_ref[sl, lp] = y_pair[sl] + _dot_tb(q_pair[sl].astype(BF16), sb)
            state = state * pe_p[ci * CHUNK:ci * CHUNK + 1, :] + _dot(sb, xt.astype(BF16)) + ht
        state_ref[p] = state


def _scan(zc3, mu, w0, wdec, a0, wa, k_k, k_a, seg, b, t, tt, rev):
    nt = t // tt
    tile_of_step = (lambda s: nt - 1 - s) if rev else (lambda s: s)
    const = lambda i, s: (0, 0)
    vec = lambda w: pl.BlockSpec((1, w), const)
    in_specs = _halo_specs(RWKV_IN, tt, t, tile_of_step) + [
        vec(RWKV_IN), vec(HEADS_W), pl.BlockSpec((LORA_W, HEADS_W), const),
        vec(HEADS_W), pl.BlockSpec((LORA_W, HEADS_W), const),
        vec(HEADS_W), vec(HEADS_W), pl.BlockSpec((HEADS_W, HEADS_W), const)]
    return pl.pallas_call(
        functools.partial(_scan_kernel, tt=tt, nt=nt, rev=rev),
        grid=(b, nt),
        in_specs=in_specs,
        out_specs=pl.BlockSpec((None, tt, HEADS_W), lambda i, s: (i, tile_of_step(s), 0)),
        out_shape=jax.ShapeDtypeStruct((b, t, HEADS_W), F32),
        scratch_shapes=[pltpu.VMEM((N_PAIRS, LANES, LANES), F32)],
        compiler_params=pltpu.CompilerParams(
            dimension_semantics=("parallel", "arbitrary"), vmem_limit_bytes=VMEM_LIMIT),
        name="rwkv_scan_bwd" if rev else "rwkv_scan_fwd",
    )(zc3, zc3, zc3, mu, w0, wdec, a0, wa, k_k, k_a, seg)


def _post_kernel(zc_ref, chp_ref, chn_ref, zb_ref, bhp_ref, bhn_ref, yf_ref, yb_ref,
                 mu_ref, a0_ref, wa_ref, wg_ref, ka_ref, rk_ref, lnw_ref, lnb_ref,
                 cw_ref, seg_ref, yc_ref, conv_ref, *, nt):
    tau = pl.program_id(1)
    seg = seg_ref[...]

    zs = _token_shift(zc_ref, chp_ref, chn_ref, mu_ref, tau, nt)
    r = zs[:, 0:HEADS_W]
    k = zs[:, HEADS_W:2 * HEADS_W]
    v = zs[:, 2 * HEADS_W:3 * HEADS_W]
    lora = zs[:, 3 * HEADS_W:RWKV_IN]
    lora_b = lora.astype(BF16)
    asum = (_sigmoid(a0_ref[0:1, :] + _dot(lora_b, wa_ref[0]))
            + _sigmoid(a0_ref[1:2, :] + _dot(lora_b, wa_ref[1])))
    gate = _dot(_sigmoid(lora).astype(BF16), wg_ref[...])
    ksum = k * (2.0 + (asum - 2.0) * ka_ref[...])
    bonus = _segsum(r * ksum * rk_ref[...], seg) * v

    y = yf_ref[...] + yb_ref[...]
    mean = _segsum(y, seg) * (1.0 / HEAD_DIM)
    yc = y - mean
    var = _segsum(yc * yc, seg) * (1.0 / HEAD_DIM)
    yn = yc * lax.rsqrt(var + LNX_EPS) * lnw_ref[...] + lnb_ref[...]
    yc_ref[...] = (yn + bonus) * gate

    zb = zb_ref[...]
    prev_row, next_row = _halo_rows(bhp_ref, bhn_ref, tau, nt)
    gated = lambda z: z[:, CONV_W:2 * CONV_W] * z[:, 2 * CONV_W:3 * CONV_W]
    u_prev, u_next = _neighbours(gated(zb), gated(prev_row), gated(next_row))
    conv = u_prev * cw_ref[0:1, :] + gated(zb) * cw_ref[1:2, :] + u_next * cw_ref[2:3, :]
    conv_ref[...] = zb[:, 0:CONV_W] * conv


def _post(zc3, zb3, yf, yb, mu, a0, wa, wg, k_a, r_k, lnw, lnb, cw, seg, b, t, tt):
    nt = t // tt
    ident = lambda s: s
    const2 = lambda i, s: (0, 0)
    const3 = lambda i, s: (0, 0, 0)
    vec = lambda w: pl.BlockSpec((1, w), const2)
    tile = lambda w: pl.BlockSpec((None, tt, w), lambda i, s: (i, s, 0))
    in_specs = (_halo_specs(RWKV_IN, tt, t, ident) + _halo_specs(CONVIN_W, tt, t, ident) + [
        tile(HEADS_W), tile(HEADS_W), vec(RWKV_IN),
        pl.BlockSpec((2, HEADS_W), const2), pl.BlockSpec((2, LORA_W, HEADS_W), const3),
        pl.BlockSpec((LORA_W, HEADS_W), const2), vec(HEADS_W), vec(HEADS_W), vec(HEADS_W),
        vec(HEADS_W), pl.BlockSpec((3, CONV_W), const2),
        pl.BlockSpec((HEADS_W, HEADS_W), const2)])
    return pl.pallas_call(
        functools.partial(_post_kernel, nt=nt),
        grid=(b, nt),
        in_specs=in_specs,
        out_specs=[tile(HEADS_W), tile(CONV_W)],
        out_shape=[jax.ShapeDtypeStruct((b, t, HEADS_W), F32),
                   jax.ShapeDtypeStruct((b, t, CONV_W), F32)],
        compiler_params=pltpu.CompilerParams(
            dimension_semantics=("parallel", "parallel"), vmem_limit_bytes=VMEM_LIMIT),
        name="rwkv_post_conv",
    )(zc3, zc3, zc3, zb3, zb3, zb3, yf, yb, mu, a0, wa, wg, k_a, r_k, lnw, lnb, cw, seg)


def _mixout_kernel(x_ref, att_ref, conv_ref, yc_ref, ga_ref, gb_ref, w_ref, gpost_ref, o_ref):
    ya = _rms(att_ref[...], ga_ref[...]).astype(BF16)
    yb = _rms(conv_ref[...], gb_ref[...]).astype(BF16)
    yc = yc_ref[...].astype(BF16)
    mix = (_dot(ya, w_ref[0:HEADS_W, :])
           + _dot(yb, w_ref[HEADS_W:HEADS_W + CONV_W, :])
           + _dot(yc, w_ref[HEADS_W + CONV_W:, :]))
    o_ref[...] = x_ref[...] + _rms(mix, gpost_ref[...])


def _mixout(x2, att, conv, yc, ga, gb, w_bf16, gpost, tm):
    n = x2.shape[0]
    row = lambda i: (i, 0)
    const = lambda i: (0, 0)
    return pl.pallas_call(
        _mixout_kernel,
        grid=(n // tm,),
        in_specs=[pl.BlockSpec((tm, D_MODEL), row), pl.BlockSpec((tm, HEADS_W), row),
                  pl.BlockSpec((tm, CONV_W), row), pl.BlockSpec((tm, HEADS_W), row),
                  pl.BlockSpec((1, HEADS_W), const), pl.BlockSpec((1, CONV_W), const),
                  pl.BlockSpec((D_MODEL, D_MODEL), const), pl.BlockSpec((1, D_MODEL), const)],
        out_specs=pl.BlockSpec((tm, D_MODEL), row),
        out_shape=jax.ShapeDtypeStruct((n, D_MODEL), F32),
        compiler_params=pltpu.CompilerParams(
            dimension_semantics=("parallel",), vmem_limit_bytes=VMEM_LIMIT),
        name="mix_out",
    )(x2, att, conv, yc, ga, gb, w_bf16, gpost)


def _ffn_kernel(x_ref, gpre_ref, w1_ref, w2_ref, gpost_ref, o_ref, *, ff_tile):
    x = x_ref[...]
    h = _rms(x, gpre_ref[...]).astype(BF16)
    acc = None
    for j in range(D_FF // ff_tile):
        sl = slice(j * ff_tile, (j + 1) * ff_tile)
        a = jnp.maximum(_dot(h, w1_ref[:, sl]), 0.0)
        part = _dot((a * a).astype(BF16), w2_ref[sl, :])
        acc = part if acc is None else acc + part
    o_ref[...] = x + _rms(acc, gpost_ref[...])


def _ffn(x2, gpre, w1_bf16, w2_bf16, gpost, tm):
    n = x2.shape[0]
    row = lambda i: (i, 0)
    const = lambda i: (0, 0)
    return pl.pallas_call(
        functools.partial(_ffn_kernel, ff_tile=1024),
        grid=(n // tm,),
        in_specs=[pl.BlockSpec((tm, D_MODEL), row), pl.BlockSpec((1, D_MODEL), const),
                  pl.BlockSpec((D_MODEL, D_FF), const), pl.BlockSpec((D_FF, D_MODEL), const),
                  pl.BlockSpec((1, D_MODEL), const)],
        out_specs=pl.BlockSpec((tm, D_MODEL), row),
        out_shape=jax.ShapeDtypeStruct((n, D_MODEL), F32),
        compiler_params=pltpu.CompilerParams(
            dimension_semantics=("parallel",), vmem_limit_bytes=VMEM_LIMIT),
        name="ffn",
    )(x2, gpre, w1_bf16, w2_bf16, gpost)


def _pad_rows(w, start):
    return jnp.zeros((LORA_W, HEADS_W), F32).at[start:start + w.shape[0]].set(w)


def _layer_params(l, p):
    row = lambda a: a.reshape(1, -1).astype(F32)
    seg = np.kron(np.eye(N_HEADS, dtype=np.float32), np.ones((HEAD_DIM, HEAD_DIM), np.float32))
    return dict(
        g_mix_pre=row(p["norm_mix_pre"][l]), g_mix_post=row(p["norm_mix_post"][l]),
        g_ffn_pre=row(p["norm_ffn_pre"][l]), g_ffn_post=row(p["norm_ffn_post"][l]),
        w_in=p["w_in"][l].astype(BF16), w_out=p["w_out"][l].astype(BF16),
        attn_g=row(p["attn_out_g"][l]), conv_w=p["conv_w"][l].astype(F32),
        conv_g=row(p["conv_out_g"][l]), mu=row(p["rwkv_mu"][l]),
        w0=p["decay_w0"][l].astype(F32), a0=p["iclr_a0"][l].astype(F32),
        wdec=jnp.stack([_pad_rows(p["decay_up"][l, d], 0) for d in range(2)]).astype(BF16),
        wa=jnp.stack([_pad_rows(p["iclr_up"][l, d], DECAY_RANK) for d in range(2)]).astype(BF16),
        wg=_pad_rows(p["gate_up"][l], DECAY_RANK + ICLR_RANK).astype(BF16),
        k_k=row(p["k_k"][l]), k_a=row(p["k_a"][l]), r_k=row(p["r_k"][l]),
        lnx_w=row(p["lnx_w"][l]), lnx_b=row(p["lnx_b"][l]),
        w1=p["ffn_w1"][l].astype(BF16), w2=p["ffn_w2"][l].astype(BF16),
        seg=jnp.asarray(seg, BF16),
    )


def _layer(x2, lp, biases, b, t):
    n = b * t
    tm = min(512, n)
    tt = min(256, t)
    zq, zb, zc = _inproj(x2, lp["g_mix_pre"], lp["w_in"], tm)
    att = _attention(zq, biases, b, t)
    zc3 = zc.reshape(b, t, RWKV_IN)
    zb3 = zb.reshape(b, t, CONVIN_W)
    ys = [_scan(zc3, lp["mu"], lp["w0"][d:d + 1], lp["wdec"][d], lp["a0"][d:d + 1], lp["wa"][d],
                lp["k_k"], lp["k_a"], lp["seg"], b, t, tt, rev=bool(d)) for d in range(2)]
    yc, conv = _post(zc3, zb3, ys[0], ys[1], lp["mu"], lp["a0"], lp["wa"], lp["wg"], lp["k_a"],
                     lp["r_k"], lp["lnx_w"], lp["lnx_b"], lp["conv_w"], lp["seg"], b, t, tt)
    x2 = _mixout(x2, att, conv.reshape(n, CONV_W), yc.reshape(n, HEADS_W), lp["attn_g"],
                 lp["conv_g"], lp["w_out"], lp["g_mix_post"], tm)
    return _ffn(x2, lp["g_ffn_pre"], lp["w1"], lp["w2"], lp["g_ffn_post"], tm)


def _trunk(x, rel_bias, layers):
    b, t, _ = x.shape
    biases = [_attn_bias(rel_bias, t, dil) for dil in DILATIONS]
    x2 = x.reshape(b * t, D_MODEL)
    for lp in layers:
        x2 = _layer(x2, lp, biases, b, t)
    return x2.reshape(b, t, D_MODEL)


def kernel(x_prompt, x_sample, rel_bias, norm_mix_pre, norm_mix_post, norm_ffn_pre, norm_ffn_post, w_in, w_out, attn_out_g, conv_w, conv_out_g, rwkv_mu, decay_w0, decay_up, iclr_a0, iclr_up, gate_up, k_k, k_a, r_k, lnx_w, lnx_b, ffn_w1, ffn_w2):
    p = dict(norm_mix_pre=norm_mix_pre, norm_mix_post=norm_mix_post, norm_ffn_pre=norm_ffn_pre,
             norm_ffn_post=norm_ffn_post, w_in=w_in, w_out=w_out, attn_out_g=attn_out_g,
             conv_w=conv_w, conv_out_g=conv_out_g, rwkv_mu=rwkv_mu, decay_w0=decay_w0,
             decay_up=decay_up, iclr_a0=iclr_a0, iclr_up=iclr_up, gate_up=gate_up, k_k=k_k,
             k_a=k_a, r_k=r_k, lnx_w=lnx_w, lnx_b=lnx_b, ffn_w1=ffn_w1, ffn_w2=ffn_w2)
    layers = [_layer_params(l, p) for l in range(w_in.shape[0])]
    return (_trunk(x_prompt, rel_bias, layers), _trunk(x_sample, rel_bias, layers))
```

```python
import functools
import math

import numpy as np
import jax
import jax.numpy as jnp
from jax import lax
from jax.experimental import pallas as pl
from jax.experimental.pallas import tpu as pltpu

F32 = jnp.float32
BF16 = jnp.bfloat16

D_MODEL = 1024
HEAD_DIM = 64
N_HEADS = 6
HEADS_W = N_HEADS * HEAD_DIM
CONV_W = 256
QKV_W = 3 * HEADS_W
CONVIN_W = 3 * CONV_W
LORA_W = 128
DECAY_RANK = 32
ICLR_RANK = 32
RWKV_IN = 3 * HEADS_W + LORA_W
IN_WIDTH = QKV_W + CONVIN_W + RWKV_IN
D_FF = 4 * D_MODEL
DILATIONS = (1, 4, 16)
KEYS_PER_SIDE = 64
N_BUCKETS = 32
BUCKET_MAX_DIST = 1024
RMS_EPS = 1e-6
LNX_EPS = 64e-5
CHUNK = 64
LANES = 128
N_PAIRS = HEADS_W // LANES
HALO_ROWS = 8
VMEM_LIMIT = 56 * 1024 * 1024


def _rms(x, g, eps=RMS_EPS):
    return x * lax.rsqrt(jnp.mean(x * x, axis=-1, keepdims=True) + eps) * g


def _sigmoid(x):
    return 1.0 / (1.0 + jnp.exp(-x))


def _dot(a, b):
    return jnp.dot(a, b, preferred_element_type=F32)


def _dot_tb(a, b):
    return lax.dot_general(a, b, (((1,), (1,)), ((), ())), preferred_element_type=F32)


def _dot_ta(a, b):
    return lax.dot_general(a, b, (((0,), (0,)), ((), ())), preferred_element_type=F32)


def _split2(x):
    hi = x.astype(BF16)
    lo = (x - hi.astype(F32)).astype(BF16)
    return hi, lo


def _split3(x):
    hi = x.astype(BF16)
    r1 = x - hi.astype(F32)
    mid = r1.astype(BF16)
    lo = (r1 - mid.astype(F32)).astype(BF16)
    return hi, mid, lo


def _segsum(x, seg_ones):
    hi, lo = _split2(x)
    return _dot(hi, seg_ones) + _dot(lo, seg_ones)


def _inproj_kernel(x_ref, g_ref, w_ref, zq_ref, zb_ref, zc_ref):
    h = _rms(x_ref[...], g_ref[...]).astype(BF16)
    zq_ref[...] = _dot(h, w_ref[:, 0:QKV_W])
    zb_ref[...] = _dot(h, w_ref[:, QKV_W:QKV_W + CONVIN_W])
    zc_ref[...] = _dot(h, w_ref[:, QKV_W + CONVIN_W:IN_WIDTH])


def _inproj(x2, g, w_bf16, tm):
    n = x2.shape[0]
    row = lambda i: (i, 0)
    const = lambda i: (0, 0)
    return pl.pallas_call(
        _inproj_kernel,
        grid=(n // tm,),
        in_specs=[pl.BlockSpec((tm, D_MODEL), row),
                  pl.BlockSpec((1, D_MODEL), const),
                  pl.BlockSpec((D_MODEL, IN_WIDTH), const)],
        out_specs=[pl.BlockSpec((tm, QKV_W), row),
                   pl.BlockSpec((tm, CONVIN_W), row),
                   pl.BlockSpec((tm, RWKV_IN), row)],
        out_shape=[jax.ShapeDtypeStruct((n, QKV_W), F32),
                   jax.ShapeDtypeStruct((n, CONVIN_W), F32),
                   jax.ShapeDtypeStruct((n, RWKV_IN), F32)],
        compiler_params=pltpu.CompilerParams(
            dimension_semantics=("parallel",), vmem_limit_bytes=VMEM_LIMIT),
        name="inproj",
    )(x2, g, w_bf16)


def _t5_bucket(rel):
    half = N_BUCKETS // 2
    max_exact = half // 2
    ret = np.where(rel > 0, half, 0)
    n = np.abs(rel)
    large = max_exact + (np.log(np.maximum(n, 1) / max_exact)
                         / np.log(BUCKET_MAX_DIST / max_exact) * (half - max_exact)).astype(np.int32)
    large = np.minimum(large, half - 1)
    return (ret + np.where(n < max_exact, n, large)).astype(np.int32)


def _attn_cfg(t, dil):
    cls_len = t // dil
    bq = min(128, cls_len)
    wk = min(bq + 2 * KEYS_PER_SIDE, cls_len)
    return cls_len, bq, wk, cls_len // bq


def _attn_bias(rel_bias, t, dil):
    _, bq, wk, _ = _attn_cfg(t, dil)
    out = []
    for shift in (0, KEYS_PER_SIDE, wk - bq):
        rel = np.arange(wk)[None, :] - np.arange(bq)[:, None] - shift
        valid = np.abs(rel) <= KEYS_PER_SIDE
        onehot = jnp.asarray(_t5_bucket(rel * dil))[:, :, None] == jnp.arange(N_BUCKETS)
        table = jnp.transpose(rel_bias.astype(F32))[:, None, None, :]
        bias = jnp.sum(jnp.where(onehot[None], table, 0.0), axis=-1)
        out.append(jnp.where(valid[None], bias, -jnp.inf))
    return jnp.stack(out)


def _attn_kernel(q_ref, k_ref, v_ref, b0_ref, b1_ref, b2_ref, o_ref, m_ref, s_ref, *, t):
    lane = lax.broadcasted_iota(jnp.int32, (1, LANES), 1)
    head0 = lane < HEAD_DIM
    bias_refs = (b0_ref, b1_ref, b2_ref)

    for di, dil in enumerate(DILATIONS):
        cls_len, bq, wk, nb = _attn_cfg(t, dil)
        bias_ref = bias_refs[di]

        def rows(start, size, dil=dil):
            if dil == 1:
                return pl.ds(start, size)
            return pl.ds(start, size, stride=dil)

        def do_block(c, blk, placement, di=di, dil=dil, bq=bq, wk=wk, cls_len=cls_len,
                     bias_ref=bias_ref, rows=rows):
            m0 = blk * bq
            if placement == 0:
                ws = 0
            elif placement == 1:
                ws = m0 - KEYS_PER_SIDE
            else:
                ws = cls_len - wk
            qrows = rows(c + dil * m0, bq)
            krows = rows(c + dil * ws, wk)
            q = q_ref[qrows, :] * (HEAD_DIM ** -0.5)
            kw = k_ref[krows, :].astype(BF16)
            vw = v_ref[krows, :].astype(BF16)
            stats = []
            for h in range(2):
                hm = head0 if h == 0 else jnp.logical_not(head0)
                qh = jnp.where(hm, q, 0.0).astype(BF16)
                logits = _dot_tb(qh, kw) + bias_ref[placement, h]
                mh = jnp.max(logits, axis=-1, keepdims=True)
                p = jnp.exp(logits - mh)
                sh = jnp.sum(p, axis=-1, keepdims=True)
                oh = _dot(p.astype(BF16), vw)
                stats.append((mh, sh, oh))
            m_blk = jnp.where(head0, stats[0][0], stats[1][0])
            s_blk = jnp.where(head0, stats[0][1], stats[1][1])
            o_blk = jnp.where(head0, stats[0][2], stats[1][2])
            if di == 0:
                m_ref[qrows, :] = m_blk
                s_ref[qrows, :] = s_blk
                o_ref[qrows, :] = o_blk
            else:
                m_old = m_ref[qrows, :]
                m_new = jnp.maximum(m_old, m_blk)
                a_old = jnp.exp(m_old - m_new)
                a_blk = jnp.exp(m_blk - m_new)
                m_ref[qrows, :] = m_new
                s_ref[qrows, :] = s_ref[qrows, :] * a_old + s_blk * a_blk
                o_ref[qrows, :] = o_ref[qrows, :] * a_old + o_blk * a_blk

        def class_body(c, carry, nb=nb, do_block=do_block):
            do_block(c, 0, 0)
            if nb > 2:
                def mid(blk, cc):
                    do_block(c, blk, 1)
                    return cc
                lax.fori_loop(1, nb - 1, mid, 0)
            if nb > 1:
                do_block(c, nb - 1, 2)
            return carry

        if dil == 1:
            class_body(0, 0)
        else:
            lax.fori_loop(0, dil, class_body, 0)

    rb = min(256, t)

    def norm_body(i, carry):
        sl = pl.ds(pl.multiple_of(i * rb, rb), rb)
        o_ref[sl, :] = o_ref[sl, :] / s_ref[sl, :]
        return carry

    lax.fori_loop(0, t // rb, norm_body, 0)


def _attention(zq, biases, b, t):
    zq3 = zq.reshape(b, t, QKV_W)
    in_specs = [pl.BlockSpec((None, t, LANES), lambda i, hp: (i, 0, hp)),
                pl.BlockSpec((None, t, LANES), lambda i, hp: (i, 0, N_PAIRS + hp)),
                pl.BlockSpec((None, t, LANES), lambda i, hp: (i, 0, 2 * N_PAIRS + hp))]
    for bias in biases:
        in_specs.append(pl.BlockSpec((3, 2) + bias.shape[2:], lambda i, hp: (0, hp, 0, 0)))
    out = pl.pallas_call(
        functools.partial(_attn_kernel, t=t),
        grid=(b, N_PAIRS),
        in_specs=in_specs,
        out_specs=pl.BlockSpec((None, t, LANES), lambda i, hp: (i, 0, hp)),
        out_shape=jax.ShapeDtypeStruct((b, t, HEADS_W), F32),
        scratch_shapes=[pltpu.VMEM((t, LANES), F32), pltpu.VMEM((t, LANES), F32)],
        compiler_params=pltpu.CompilerParams(
            dimension_semantics=("parallel", "parallel"), vmem_limit_bytes=VMEM_LIMIT),
        name="dilated_attention",
    )(zq3, zq3, zq3, *biases)
    return out.reshape(b * t, HEADS_W)


def _neighbours(main, prev_row, next_row):
    tt = main.shape[0]
    row = lax.broadcasted_iota(jnp.int32, main.shape, 0)
    prev = jnp.where(row == 0, prev_row, pltpu.roll(main, 1, 0))
    nxt = jnp.where(row == tt - 1, next_row, pltpu.roll(main, tt - 1, 0))
    return prev, nxt


def _halo_rows(hp_ref, hn_ref, tau, nt):
    prev_row = jnp.where(tau > 0, hp_ref[HALO_ROWS - 1:HALO_ROWS, :], 0.0)
    next_row = jnp.where(tau < nt - 1, hn_ref[0:1, :], 0.0)
    return prev_row, next_row


def _token_shift(zc_ref, hp_ref, hn_ref, mu_ref, tau, nt):
    zc = zc_ref[...]
    prev_row, next_row = _halo_rows(hp_ref, hn_ref, tau, nt)
    prev, nxt = _neighbours(zc, prev_row, next_row)
    return zc + (0.5 * (prev + nxt) - zc) * mu_ref[...]


def _halo_specs(width, tt, t, tile_of_step):
    per = tt // HALO_ROWS
    last = t // HALO_ROWS - 1
    main = pl.BlockSpec((None, tt, width), lambda i, s: (i, tile_of_step(s), 0))
    prev = pl.BlockSpec((None, HALO_ROWS, width),
                        lambda i, s: (i, jnp.maximum(tile_of_step(s) * per - 1, 0), 0))
    nxt = pl.BlockSpec((None, HALO_ROWS, width),
                       lambda i, s: (i, jnp.minimum((tile_of_step(s) + 1) * per, last), 0))
    return [main, prev, nxt]


def _scan_kernel(zc_ref, hp_ref, hn_ref, mu_ref, w0_ref, wdec_ref, a0_ref, wa_ref,
                 kk_ref, ka_ref, seg_ref, y_ref, state_ref, *, tt, nt, rev):
    step = pl.program_id(1)
    tau = (nt - 1 - step) if rev else step

    @pl.when(step == 0)
    def _():
        state_ref[...] = jnp.zeros_like(state_ref)

    zs = _token_shift(zc_ref, hp_ref, hn_ref, mu_ref, tau, nt)
    r = zs[:, 0:HEADS_W]
    k = zs[:, HEADS_W:2 * HEADS_W]
    v = zs[:, 2 * HEADS_W:3 * HEADS_W]
    lora = zs[:, 3 * HEADS_W:RWKV_IN]

    xdec = w0_ref[...] + _dot(jnp.tanh(lora).astype(BF16), wdec_ref[...])
    lw = (-math.exp(-0.5)) * _sigmoid(xdec)
    asig = _sigmoid(a0_ref[...] + _dot(lora.astype(BF16), wa_ref[...]))
    kk = k * kk_ref[...]
    kk = kk / jnp.maximum(jnp.sqrt(_segsum(kk * kk, seg_ref[...])), 1e-12)
    kdir = k * (1.0 + (asig - 1.0) * ka_ref[...])
    bvec = kk * asig

    n_chunks = tt // CHUNK
    pack = lambda x: jnp.concatenate(
        [x[c * CHUNK:(c + 1) * CHUNK] for c in range(n_chunks)], axis=1)
    lw_p, r_p, kd_p, bv_p, kk_p, v_p = (pack(x) for x in (lw, r, kdir, bvec, kk, v))

    trow = lax.broadcasted_iota(jnp.int32, (CHUNK, CHUNK), 0)
    tcol = lax.broadcasted_iota(jnp.int32, (CHUNK, CHUNK), 1)
    tri = jnp.where((tcol >= trow) if rev else (tcol <= trow), 1.0, 0.0).astype(BF16)
    lw3 = _split3(lw_p)
    cum = _dot(tri, lw3[0]) + _dot(tri, lw3[1]) + _dot(tri, lw3[2])
    last = 0 if rev else CHUNK - 1
    tot = cum[last:last + 1, :]
    e_neg = jnp.exp(-cum)
    e_end = jnp.exp(tot - cum)
    r_t = r_p * jnp.exp(cum)
    k_t = kd_p * e_neg
    b_t = bv_p * e_neg
    a_t = -kk_p * jnp.exp(cum - lw_p)
    b_e = bv_p * e_end
    k_e = kd_p * e_end
    p_end = jnp.exp(tot)

    gw = 2 * LANES
    grow = lax.broadcasted_iota(jnp.int32, (gw, gw), 0)
    gcol = lax.broadcasted_iota(jnp.int32, (gw, gw), 1)
    bdmask = (grow // CHUNK) == (gcol // HEAD_DIM)
    tpos = lax.broadcasted_iota(jnp.int32, (CHUNK, gw), 0)
    spos = lax.broadcasted_iota(jnp.int32, (CHUNK, gw), 1) % CHUNK
    if rev:
        incl, strict = spos >= tpos, spos > tpos
    else:
        incl, strict = spos <= tpos, spos < tpos
    eye_g = jnp.where(spos == tpos, 1.0, 0.0)
    prow = lax.broadcasted_iota(jnp.int32, (LANES, LANES), 0)
    pcol = lax.broadcasted_iota(jnp.int32, (LANES, LANES), 1)
    pair_diag = (prow // HEAD_DIM) == (pcol // HEAD_DIM)

    def bd(xb):
        return jnp.where(bdmask, jnp.concatenate([xb] * 4, axis=0), jnp.zeros((), BF16))

    order = range(n_chunks - 1, -1, -1) if rev else range(n_chunks)

    def piece(x, p, c):
        lane0 = c * HEADS_W + p * LANES
        return x[:, lane0:lane0 + LANES]

    def grp(x, key):
        p, g = key
        return jnp.concatenate([piece(x, p, 2 * g), piece(x, p, 2 * g + 1)], axis=1)

    groups = [(p, g) for p in range(N_PAIRS) for g in range(n_chunks // 2)]
    pairs = range(N_PAIRS)
    rg = {k: grp(r_t, k) for k in groups}
    ag_b = {k: grp(a_t, k).astype(BF16) for k in groups}
    lhs = {k: jnp.concatenate([ag_b[k], rg[k].astype(BF16)], axis=0) for k in groups}
    s1 = {k: _dot_tb(lhs[k], bd(grp(b_t, k).astype(BF16))) for k in groups}
    s2 = {k: _dot_tb(lhs[k], bd(grp(k_t, k).astype(BF16))) for k in groups}
    a_ab = {k: jnp.where(strict, s1[k][:CHUNK], 0.0) for k in groups}
    a_rb = {k: jnp.where(incl, s1[k][CHUNK:], 0.0).astype(BF16) for k in groups}
    a_ak = {k: jnp.where(strict, s2[k][:CHUNK], 0.0).astype(BF16) for k in groups}
    a_rk = {k: jnp.where(incl, s2[k][CHUNK:], 0.0).astype(BF16) for k in groups}
    xb = {k: a_ab[k].astype(BF16) for k in groups}
    tinv = {k: eye_g + a_ab[k] for k in groups}
    xbd = {k: bd(xb[k]) for k in groups}
    for _ in range(5):
        xb = {k: _dot(xb[k], xbd[k]).astype(BF16) for k in groups}
        xbd = {k: bd(xb[k]) for k in groups}
        tinv = {k: tinv[k] + _dot(tinv[k].astype(BF16), xbd[k]) for k in groups}
    tb = {k: tinv[k].astype(BF16) for k in groups}
    vbd = {k: bd(grp(v_p, k).astype(BF16)) for k in groups}
    w2 = {k: _dot(a_ak[k], vbd[k]).astype(BF16) for k in groups}
    ta = {k: _dot(tb[k], bd(ag_b[k])) for k in groups}
    tv = {k: _dot(tb[k], bd(w2[k])) for k in groups}
    qg = {k: rg[k] + _dot(a_rb[k], bd(ta[k].astype(BF16))) for k in groups}
    yg = {k: _dot(a_rb[k], bd(tv[k].astype(BF16))) + _dot(a_rk[k], vbd[k]) for k in groups}

    def chunk_of(d, p, c):
        half = slice((c % 2) * LANES, (c % 2 + 1) * LANES)
        return d[(p, c // 2)][:, half]

    pcs = [(p, c) for c in order for p in pairs]
    be = {pc: piece(b_e, *pc) for pc in pcs}
    xt = {pc: jnp.where(pair_diag, _dot_ta(chunk_of(ta, *pc).astype(BF16),
                                           be[pc].astype(BF16)), 0.0).astype(BF16)
          for pc in pcs}
    ht = {pc: jnp.where(pair_diag, _dot_ta(
        jnp.concatenate([chunk_of(tv, *pc), piece(v_p, *pc)], axis=0).astype(BF16),
        jnp.concatenate([be[pc], piece(k_e, *pc)], axis=0).astype(BF16)), 0.0) for pc in pcs}

    state = {p: state_ref[p] for p in pairs}
    for ci in order:
        sb = {p: state[p].astype(BF16) for p in pairs}
        for p in pairs:
            y_ref[ci * CHUNK:(ci + 1) * CHUNK, p * LANES:(p + 1) * LANES] = (
                chunk_of(yg, p, ci) + _dot_tb(chunk_of(qg, p, ci).astype(BF16), sb[p]))
        state = {p: state[p] * piece(p_end, p, ci) + _dot(sb[p], xt[(p, ci)]) + ht[(p, ci)]
                 for p in pairs}
    for p in pairs:
        state_ref[p] = state[p]


def _scan(zc3, mu, w0, wdec, a0, wa, k_k, k_a, seg, b, t, tt, rev):
    nt = t // tt
    tile_of_step = (lambda s: nt - 1 - s) if rev else (lambda s: s)
    const = lambda i, s: (0, 0)
    vec = lambda w: pl.BlockSpec((1, w), const)
    in_specs = _halo_specs(RWKV_IN, tt, t, tile_of_step) + [
        vec(RWKV_IN), vec(HEADS_W), pl.BlockSpec((LORA_W, HEADS_W), const),
        vec(HEADS_W), pl.BlockSpec((LORA_W, HEADS_W), const),
        vec(HEADS_W), vec(HEADS_W), pl.BlockSpec((HEADS_W, HEADS_W), const)]
    return pl.pallas_call(
        functools.partial(_scan_kernel, tt=tt, nt=nt, rev=rev),
        grid=(b, nt),
        in_specs=in_specs,
        out_specs=pl.BlockSpec((None, tt, HEADS_W), lambda i, s: (i, tile_of_step(s), 0)),
        out_shape=jax.ShapeDtypeStruct((b, t, HEADS_W), F32),
        scratch_shapes=[pltpu.VMEM((N_PAIRS, LANES, LANES), F32)],
        compiler_params=pltpu.CompilerParams(
            dimension_semantics=("parallel", "arbitrary"), vmem_limit_bytes=VMEM_LIMIT),
        name="rwkv_scan_bwd" if rev else "rwkv_scan_fwd",
    )(zc3, zc3, zc3, mu, w0, wdec, a0, wa, k_k, k_a, seg)


def _post_kernel(zc_ref, chp_ref, chn_ref, zb_ref, bhp_ref, bhn_ref, yf_ref, yb_ref,
                 mu_ref, a0_ref, wa_ref, wg_ref, ka_ref, rk_ref, lnw_ref, lnb_ref,
                 cw_ref, seg_ref, yc_ref, conv_ref, *, nt):
    tau = pl.program_id(1)
    seg = seg_ref[...]

    zs = _token_shift(zc_ref, chp_ref, chn_ref, mu_ref, tau, nt)
    r = zs[:, 0:HEADS_W]
    k = zs[:, HEADS_W:2 * HEADS_W]
    v = zs[:, 2 * HEADS_W:3 * HEADS_W]
    lora = zs[:, 3 * HEADS_W:RWKV_IN]
    lora_b = lora.astype(BF16)
    asum = (_sigmoid(a0_ref[0:1, :] + _dot(lora_b, wa_ref[0]))
            + _sigmoid(a0_ref[1:2, :] + _dot(lora_b, wa_ref[1])))
    gate = _dot(_sigmoid(lora).astype(BF16), wg_ref[...])
    ksum = k * (2.0 + (asum - 2.0) * ka_ref[...])
    bonus = _segsum(r * ksum * rk_ref[...], seg) * v

    y = yf_ref[...] + yb_ref[...]
    mean = _segsum(y, seg) * (1.0 / HEAD_DIM)
    yc = y - mean
    var = _segsum(yc * yc, seg) * (1.0 / HEAD_DIM)
    yn = yc * lax.rsqrt(var + LNX_EPS) * lnw_ref[...] + lnb_ref[...]
    yc_ref[...] = (yn + bonus) * gate

    zb = zb_ref[...]
    prev_row, next_row = _halo_rows(bhp_ref, bhn_ref, tau, nt)
    gated = lambda z: z[:, CONV_W:2 * CONV_W] * z[:, 2 * CONV_W:3 * CONV_W]
    u_prev, u_next = _neighbours(gated(zb), gated(prev_row), gated(next_row))
    conv = u_prev * cw_ref[0:1, :] + gated(zb) * cw_ref[1:2, :] + u_next * cw_ref[2:3, :]
    conv_ref[...] = zb[:, 0:CONV_W] * conv


def _post(zc3, zb3, yf, yb, mu, a0, wa, wg, k_a, r_k, lnw, lnb, cw, seg, b, t, tt):
    nt = t // tt
    ident = lambda s: s
    const2 = lambda i, s: (0, 0)
    const3 = lambda i, s: (0, 0, 0)
    vec = lambda w: pl.BlockSpec((1, w), const2)
    tile = lambda w: pl.BlockSpec((None, tt, w), lambda i, s: (i, s, 0))
    in_specs = (_halo_specs(RWKV_IN, tt, t, ident) + _halo_specs(CONVIN_W, tt, t, ident) + [
        tile(HEADS_W), tile(HEADS_W), vec(RWKV_IN),
        pl.BlockSpec((2, HEADS_W), const2), pl.BlockSpec((2, LORA_W, HEADS_W), const3),
        pl.BlockSpec((LORA_W, HEADS_W), const2), vec(HEADS_W), vec(HEADS_W), vec(HEADS_W),
        vec(HEADS_W), pl.BlockSpec((3, CONV_W), const2),
        pl.BlockSpec((HEADS_W, HEADS_W), const2)])
    return pl.pallas_call(
        functools.partial(_post_kernel, nt=nt),
        grid=(b, nt),
        in_specs=in_specs,
        out_specs=[tile(HEADS_W), tile(CONV_W)],
        out_shape=[jax.ShapeDtypeStruct((b, t, HEADS_W), F32),
                   jax.ShapeDtypeStruct((b, t, CONV_W), F32)],
        compiler_params=pltpu.CompilerParams(
            dimension_semantics=("parallel", "parallel"), vmem_limit_bytes=VMEM_LIMIT),
        name="rwkv_post_conv",
    )(zc3, zc3, zc3, zb3, zb3, zb3, yf, yb, mu, a0, wa, wg, k_a, r_k, lnw, lnb, cw, seg)


def _mixout_kernel(x_ref, att_ref, conv_ref, yc_ref, ga_ref, gb_ref, w_ref, gpost_ref, o_ref):
    ya = _rms(att_ref[...], ga_ref[...]).astype(BF16)
    yb = _rms(conv_ref[...], gb_ref[...]).astype(BF16)
    yc = yc_ref[...].astype(BF16)
    mix = (_dot(ya, w_ref[0:HEADS_W, :])
           + _dot(yb, w_ref[HEADS_W:HEADS_W + CONV_W, :])
           + _dot(yc, w_ref[HEADS_W + CONV_W:, :]))
    o_ref[...] = x_ref[...] + _rms(mix, gpost_ref[...])


def _mixout(x2, att, conv, yc, ga, gb, w_bf16, gpost, tm):
    n = x2.shape[0]
    row = lambda i: (i, 0)
    const = lambda i: (0, 0)
    return pl.pallas_call(
        _mixout_kernel,
        grid=(n // tm,),
        in_specs=[pl.BlockSpec((tm, D_MODEL), row), pl.BlockSpec((tm, HEADS_W), row),
                  pl.BlockSpec((tm, CONV_W), row), pl.BlockSpec((tm, HEADS_W), row),
                  pl.BlockSpec((1, HEADS_W), const), pl.BlockSpec((1, CONV_W), const),
                  pl.BlockSpec((D_MODEL, D_MODEL), const), pl.BlockSpec((1, D_MODEL), const)],
        out_specs=pl.BlockSpec((tm, D_MODEL), row),
        out_shape=jax.ShapeDtypeStruct((n, D_MODEL), F32),
        compiler_params=pltpu.CompilerParams(
            dimension_semantics=("parallel",), vmem_limit_bytes=VMEM_LIMIT),
        name="mix_out",
    )(x2, att, conv, yc, ga, gb, w_bf16, gpost)


def _ffn_kernel(x_ref, gpre_ref, w1_ref, w2_ref, gpost_ref, o_ref, *, ff_tile):
    x = x_ref[...]
    h = _rms(x, gpre_ref[...]).astype(BF16)
    acc = None
    for j in range(D_FF // ff_tile):
        sl = slice(j * ff_tile, (j + 1) * ff_tile)
        a = jnp.maximum(_dot(h, w1_ref[:, sl]), 0.0)
        part = _dot((a * a).astype(BF16), w2_ref[sl, :])
        acc = part if acc is None else acc + part
    o_ref[...] = x + _rms(acc, gpost_ref[...])


def _ffn(x2, gpre, w1_bf16, w2_bf16, gpost, tm):
    n = x2.shape[0]
    row = lambda i: (i, 0)
    const = lambda i: (0, 0)
    return pl.pallas_call(
        functools.partial(_ffn_kernel, ff_tile=1024),
        grid=(n // tm,),
        in_specs=[pl.BlockSpec((tm, D_MODEL), row), pl.BlockSpec((1, D_MODEL), const),
                  pl.BlockSpec((D_MODEL, D_FF), const), pl.BlockSpec((D_FF, D_MODEL), const),
                  pl.BlockSpec((1, D_MODEL), const)],
        out_specs=pl.BlockSpec((tm, D_MODEL), row),
        out_shape=jax.ShapeDtypeStruct((n, D_MODEL), F32),
        compiler_params=pltpu.CompilerParams(
            dimension_semantics=("parallel",), vmem_limit_bytes=VMEM_LIMIT),
        name="ffn",
    )(x2, gpre, w1_bf16, w2_bf16, gpost)


def _pad_rows(w, start):
    return jnp.zeros((LORA_W, HEADS_W), F32).at[start:start + w.shape[0]].set(w)


def _layer_params(l, p):
    row = lambda a: a.reshape(1, -1).astype(F32)
    seg = np.kron(np.eye(N_HEADS, dtype=np.float32), np.ones((HEAD_DIM, HEAD_DIM), np.float32))
    return dict(
        g_mix_pre=row(p["norm_mix_pre"][l]), g_mix_post=row(p["norm_mix_post"][l]),
        g_ffn_pre=row(p["norm_ffn_pre"][l]), g_ffn_post=row(p["norm_ffn_post"][l]),
        w_in=p["w_in"][l].astype(BF16), w_out=p["w_out"][l].astype(BF16),
        attn_g=row(p["attn_out_g"][l]), conv_w=p["conv_w"][l].astype(F32),
        conv_g=row(p["conv_out_g"][l]), mu=row(p["rwkv_mu"][l]),
        w0=p["decay_w0"][l].astype(F32), a0=p["iclr_a0"][l].astype(F32),
        wdec=jnp.stack([_pad_rows(p["decay_up"][l, d], 0) for d in range(2)]).astype(BF16),
        wa=jnp.stack([_pad_rows(p["iclr_up"][l, d], DECAY_RANK) for d in range(2)]).astype(BF16),
        wg=_pad_rows(p["gate_up"][l], DECAY_RANK + ICLR_RANK).astype(BF16),
        k_k=row(p["k_k"][l]), k_a=row(p["k_a"][l]), r_k=row(p["r_k"][l]),
        lnx_w=row(p["lnx_w"][l]), lnx_b=row(p["lnx_b"][l]),
        w1=p["ffn_w1"][l].astype(BF16), w2=p["ffn_w2"][l].astype(BF16),
        seg=jnp.asarray(seg, BF16),
    )


def _layer(x2, lp, biases, b, t):
    n = b * t
    tm = min(512, n)
    tt = min(256, t)
    zq, zb, zc = _inproj(x2, lp["g_mix_pre"], lp["w_in"], tm)
    att = _attention(zq, biases, b, t)
    zc3 = zc.reshape(b, t, RWKV_IN)
    zb3 = zb.reshape(b, t, CONVIN_W)
    ys = [_scan(zc3, lp["mu"], lp["w0"][d:d + 1], lp["wdec"][d], lp["a0"][d:d + 1], lp["wa"][d],
                lp["k_k"], lp["k_a"], lp["seg"], b, t, tt, rev=bool(d)) for d in range(2)]
    yc, conv = _post(zc3, zb3, ys[0], ys[1], lp["mu"], lp["a0"], lp["wa"], lp["wg"], lp["k_a"],
                     lp["r_k"], lp["lnx_w"], lp["lnx_b"], lp["conv_w"], lp["seg"], b, t, tt)
    x2 = _mixout(x2, att, conv.reshape(n, CONV_W), yc.reshape(n, HEADS_W), lp["attn_g"],
                 lp["conv_g"], lp["w_out"], lp["g_mix_post"], tm)
    return _ffn(x2, lp["g_ffn_pre"], lp["w1"], lp["w2"], lp["g_ffn_post"], tm)


def _trunk(x, rel_bias, layers):
    b, t, _ = x.shape
    biases = [_attn_bias(rel_bias, t, dil) for dil in DILATIONS]
    x2 = x.reshape(b * t, D_MODEL)
    for lp in layers:
        x2 = _layer(x2, lp, biases, b, t)
    return x2.reshape(b, t, D_MODEL)


def kernel(x_prompt, x_sample, rel_bias, norm_mix_pre, norm_mix_post, norm_ffn_pre, norm_ffn_post, w_in, w_out, attn_out_g, conv_w, conv_out_g, rwkv_mu, decay_w0, decay_up, iclr_a0, iclr_up, gate_up, k_k, k_a, r_k, lnx_w, lnx_b, ffn_w1, ffn_w2):
    p = dict(norm_mix_pre=norm_mix_pre, norm_mix_post=norm_mix_post, norm_ffn_pre=norm_ffn_pre,
             norm_ffn_post=norm_ffn_post, w_in=w_in, w_out=w_out, attn_out_g=attn_out_g,
             conv_w=conv_w, conv_out_g=conv_out_g, rwkv_mu=rwkv_mu, decay_w0=decay_w0,
             decay_up=decay_up, iclr_a0=iclr_a0, iclr_up=iclr_up, gate_up=gate_up, k_k=k_k,
             k_a=k_a, r_k=r_k, lnx_w=lnx_w, lnx_b=lnx_b, ffn_w1=ffn_w1, ffn_w2=ffn_w2)
    layers = [_layer_params(l, p) for l in range(w_in.shape[0])]
    return (_trunk(x_prompt, rel_bias, layers), _trunk(x_sample, rel_bias, layers))
```

```python
import functools
import math

import numpy as np
import jax
import jax.numpy as jnp
from jax import lax
from jax.experimental import pallas as pl
from jax.experimental.pallas import tpu as pltpu

F32 = jnp.float32
BF16 = jnp.bfloat16

D_MODEL = 1024
HEAD_DIM = 64
N_HEADS = 6
HEADS_W = N_HEADS * HEAD_DIM
CONV_W = 256
QKV_W = 3 * HEADS_W
CONVIN_W = 3 * CONV_W
LORA_W = 128
DECAY_RANK = 32
ICLR_RANK = 32
RWKV_IN = 3 * HEADS_W + LORA_W
IN_WIDTH = QKV_W + CONVIN_W + RWKV_IN
D_FF = 4 * D_MODEL
DILATIONS = (1, 4, 16)
KEYS_PER_SIDE = 64
ATTN_UNROLL = 4
N_BUCKETS = 32
BUCKET_MAX_DIST = 1024
RMS_EPS = 1e-6
LNX_EPS = 64e-5
CHUNK = 64
LANES = 128
N_PAIRS = HEADS_W // LANES
HALO_ROWS = 8
VMEM_LIMIT = 56 * 1024 * 1024


def _rms(x, g, eps=RMS_EPS):
    return x * lax.rsqrt(jnp.mean(x * x, axis=-1, keepdims=True) + eps) * g


def _sigmoid(x):
    return 1.0 / (1.0 + jnp.exp(-x))


def _dot(a, b):
    return jnp.dot(a, b, preferred_element_type=F32)


def _dot_tb(a, b):
    return lax.dot_general(a, b, (((1,), (1,)), ((), ())), preferred_element_type=F32)


def _dot_ta(a, b):
    return lax.dot_general(a, b, (((0,), (0,)), ((), ())), preferred_element_type=F32)


def _split2(x):
    hi = x.astype(BF16)
    lo = (x - hi.astype(F32)).astype(BF16)
    return hi, lo


def _split3(x):
    hi = x.astype(BF16)
    r1 = x - hi.astype(F32)
    mid = r1.astype(BF16)
    lo = (r1 - mid.astype(F32)).astype(BF16)
    return hi, mid, lo


def _segsum(x, seg_ones):
    hi, lo = _split2(x)
    return _dot(hi, seg_ones) + _dot(lo, seg_ones)


def _inproj_kernel(x_ref, g_ref, w_ref, zq_ref, zb_ref, zc_ref):
    h = _rms(x_ref[...], g_ref[...]).astype(BF16)
    zq_ref[...] = _dot(h, w_ref[:, 0:QKV_W])
    zb_ref[...] = _dot(h, w_ref[:, QKV_W:QKV_W + CONVIN_W])
    zc_ref[...] = _dot(h, w_ref[:, QKV_W + CONVIN_W:IN_WIDTH])


def _inproj(x2, g, w_bf16, tm):
    n = x2.shape[0]
    row = lambda i: (i, 0)
    const = lambda i: (0, 0)
    return pl.pallas_call(
        _inproj_kernel,
        grid=(n // tm,),
        in_specs=[pl.BlockSpec((tm, D_MODEL), row),
                  pl.BlockSpec((1, D_MODEL), const),
                  pl.BlockSpec((D_MODEL, IN_WIDTH), const)],
        out_specs=[pl.BlockSpec((tm, QKV_W), row),
                   pl.BlockSpec((tm, CONVIN_W), row),
                   pl.BlockSpec((tm, RWKV_IN), row)],
        out_shape=[jax.ShapeDtypeStruct((n, QKV_W), F32),
                   jax.ShapeDtypeStruct((n, CONVIN_W), F32),
                   jax.ShapeDtypeStruct((n, RWKV_IN), F32)],
        compiler_params=pltpu.CompilerParams(
            dimension_semantics=("parallel",), vmem_limit_bytes=VMEM_LIMIT),
        name="inproj",
    )(x2, g, w_bf16)


def _t5_bucket(rel):
    half = N_BUCKETS // 2
    max_exact = half // 2
    ret = np.where(rel > 0, half, 0)
    n = np.abs(rel)
    large = max_exact + (np.log(np.maximum(n, 1) / max_exact)
                         / np.log(BUCKET_MAX_DIST / max_exact) * (half - max_exact)).astype(np.int32)
    large = np.minimum(large, half - 1)
    return (ret + np.where(n < max_exact, n, large)).astype(np.int32)


def _attn_cfg(t, dil):
    cls_len = t // dil
    bq = min(128, cls_len)
    wk = min(bq + 2 * KEYS_PER_SIDE, cls_len)
    return cls_len, bq, wk, cls_len // bq


def _attn_bias(rel_bias, t, dil):
    _, bq, wk, _ = _attn_cfg(t, dil)
    out = []
    for shift in (0, KEYS_PER_SIDE, wk - bq):
        rel = np.arange(wk)[None, :] - np.arange(bq)[:, None] - shift
        valid = np.abs(rel) <= KEYS_PER_SIDE
        onehot = jnp.asarray(_t5_bucket(rel * dil))[:, :, None] == jnp.arange(N_BUCKETS)
        table = jnp.transpose(rel_bias.astype(F32))[:, None, None, :]
        bias = jnp.sum(jnp.where(onehot[None], table, 0.0), axis=-1)
        out.append(jnp.where(valid[None], bias, -jnp.inf))
    return jnp.stack(out)


def _attn_kernel(q_ref, k_ref, v_ref, b0_ref, b1_ref, b2_ref, o_ref, m_ref, s_ref, *, t):
    lane = lax.broadcasted_iota(jnp.int32, (1, LANES), 1)
    head0 = lane < HEAD_DIM
    bias_refs = (b0_ref, b1_ref, b2_ref)

    for di, dil in enumerate(DILATIONS):
        cls_len, bq, wk, nb = _attn_cfg(t, dil)
        bias_ref = bias_refs[di]

        def rows(start, size, dil=dil):
            if dil == 1:
                return pl.ds(start, size)
            return pl.ds(start, size, stride=dil)

        def do_blocks(blocks, di=di, dil=dil, bq=bq, wk=wk, cls_len=cls_len,
                      bias_ref=bias_ref, rows=rows):
            qrows, q, kw, vw = [], [], [], []
            for c, blk, placement in blocks:
                m0 = blk * bq
                ws = (0, m0 - KEYS_PER_SIDE, cls_len - wk)[placement]
                qrows.append(rows(c + dil * m0, bq))
                krows = rows(c + dil * ws, wk)
                q.append(q_ref[qrows[-1], :] * (HEAD_DIM ** -0.5))
                kw.append(k_ref[krows, :].astype(BF16))
                vw.append(v_ref[krows, :].astype(BF16))
            items = [(i, h) for i in range(len(blocks)) for h in range(2)]
            hmask = (head0, jnp.logical_not(head0))
            logits = {(i, h): _dot_tb(jnp.where(hmask[h], q[i], 0.0).astype(BF16), kw[i])
                      + bias_ref[blocks[i][2], h] for i, h in items}
            mh = {k: jnp.max(logits[k], axis=-1, keepdims=True) for k in items}
            p = {k: jnp.exp(logits[k] - mh[k]) for k in items}
            sh = {k: jnp.sum(p[k], axis=-1, keepdims=True) for k in items}
            oh = {(i, h): _dot(p[(i, h)].astype(BF16), vw[i]) for i, h in items}
            for i in range(len(blocks)):
                m_blk = jnp.where(head0, mh[(i, 0)], mh[(i, 1)])
                s_blk = jnp.where(head0, sh[(i, 0)], sh[(i, 1)])
                o_blk = jnp.where(head0, oh[(i, 0)], oh[(i, 1)])
                if di == 0:
                    m_ref[qrows[i], :] = m_blk
                    s_ref[qrows[i], :] = s_blk
                    o_ref[qrows[i], :] = o_blk
                else:
                    m_old = m_ref[qrows[i], :]
                    m_new = jnp.maximum(m_old, m_blk)
                    a_old = jnp.exp(m_old - m_new)
                    a_blk = jnp.exp(m_blk - m_new)
                    m_ref[qrows[i], :] = m_new
                    s_ref[qrows[i], :] = s_ref[qrows[i], :] * a_old + s_blk * a_blk
                    o_ref[qrows[i], :] = o_ref[qrows[i], :] * a_old + o_blk * a_blk

        def loop(n, body):
            if n == 1:
                body(0)
            else:
                lax.fori_loop(0, n, lambda i, carry: (body(i), carry)[1], 0)

        edge = [(0, 0)] + ([(nb - 1, 2)] if nb > 1 else [])
        cb = math.gcd(dil, max(1, ATTN_UNROLL // len(edge)))
        loop(dil // cb, lambda i, edge=edge, cb=cb, do_blocks=do_blocks: do_blocks(
            [(i * cb + j, blk, placement) for j in range(cb) for blk, placement in edge]))
        n_mid = max(nb - 2, 0)
        if n_mid:
            u = max(d for d in range(1, ATTN_UNROLL + 1) if n_mid % d == 0)
            per = n_mid // u
            loop(dil * per, lambda i, u=u, per=per, do_blocks=do_blocks: do_blocks(
                [(i // per, 1 + (i % per) * u + j, 1) for j in range(u)]))

    rb = min(256, t)

    def norm_body(i, carry):
        sl = pl.ds(pl.multiple_of(i * rb, rb), rb)
        o_ref[sl, :] = o_ref[sl, :] / s_ref[sl, :]
        return carry

    lax.fori_loop(0, t // rb, norm_body, 0)


def _attention(zq, biases, b, t):
    zq3 = zq.reshape(b, t, QKV_W)
    in_specs = [pl.BlockSpec((None, t, LANES), lambda i, hp: (i, 0, hp)),
                pl.BlockSpec((None, t, LANES), lambda i, hp: (i, 0, N_PAIRS + hp)),
                pl.BlockSpec((None, t, LANES), lambda i, hp: (i, 0, 2 * N_PAIRS + hp))]
    for bias in biases:
        in_specs.append(pl.BlockSpec((3, 2) + bias.shape[2:], lambda i, hp: (0, hp, 0, 0)))
    out = pl.pallas_call(
        functools.partial(_attn_kernel, t=t),
        grid=(b, N_PAIRS),
        in_specs=in_specs,
        out_specs=pl.BlockSpec((None, t, LANES), lambda i, hp: (i, 0, hp)),
        out_shape=jax.ShapeDtypeStruct((b, t, HEADS_W), F32),
        scratch_shapes=[pltpu.VMEM((t, LANES), F32), pltpu.VMEM((t, LANES), F32)],
        compiler_params=pltpu.CompilerParams(
            dimension_semantics=("parallel", "parallel"), vmem_limit_bytes=VMEM_LIMIT),
        name="dilated_attention",
    )(zq3, zq3, zq3, *biases)
    return out.reshape(b * t, HEADS_W)


def _neighbours(main, prev_row, next_row):
    tt = main.shape[0]
    row = lax.broadcasted_iota(jnp.int32, main.shape, 0)
    prev = jnp.where(row == 0, prev_row, pltpu.roll(main, 1, 0))
    nxt = jnp.where(row == tt - 1, next_row, pltpu.roll(main, tt - 1, 0))
    return prev, nxt


def _halo_rows(hp_ref, hn_ref, tau, nt):
    prev_row = jnp.where(tau > 0, hp_ref[HALO_ROWS - 1:HALO_ROWS, :], 0.0)
    next_row = jnp.where(tau < nt - 1, hn_ref[0:1, :], 0.0)
    return prev_row, next_row


def _token_shift(zc_ref, hp_ref, hn_ref, mu_ref, tau, nt):
    zc = zc_ref[...]
    prev_row, next_row = _halo_rows(hp_ref, hn_ref, tau, nt)
    prev, nxt = _neighbours(zc, prev_row, next_row)
    return zc + (0.5 * (prev + nxt) - zc) * mu_ref[...]


def _halo_specs(width, tt, t, tile_of_step):
    per = tt // HALO_ROWS
    last = t // HALO_ROWS - 1
    main = pl.BlockSpec((None, tt, width), lambda i, s: (i, tile_of_step(s), 0))
    prev = pl.BlockSpec((None, HALO_ROWS, width),
                        lambda i, s: (i, jnp.maximum(tile_of_step(s) * per - 1, 0), 0))
    nxt = pl.BlockSpec((None, HALO_ROWS, width),
                       lambda i, s: (i, jnp.minimum((tile_of_step(s) + 1) * per, last), 0))
    return [main, prev, nxt]


def _scan_kernel(zc_ref, hp_ref, hn_ref, mu_ref, w0_ref, wdec_ref, a0_ref, wa_ref,
                 kk_ref, ka_ref, seg_ref, y_ref, state_ref, *, tt, nt, rev):
    step = pl.program_id(1)
    tau = (nt - 1 - step) if rev else step

    @pl.when(step == 0)
    def _():
        state_ref[...] = jnp.zeros_like(state_ref)

    zs = _token_shift(zc_ref, hp_ref, hn_ref, mu_ref, tau, nt)
    r = zs[:, 0:HEADS_W]
    k = zs[:, HEADS_W:2 * HEADS_W]
    v = zs[:, 2 * HEADS_W:3 * HEADS_W]
    lora = zs[:, 3 * HEADS_W:RWKV_IN]

    xdec = w0_ref[...] + _dot(jnp.tanh(lora).astype(BF16), wdec_ref[...])
    lw = (-math.exp(-0.5)) * _sigmoid(xdec)
    asig = _sigmoid(a0_ref[...] + _dot(lora.astype(BF16), wa_ref[...]))
    kk = k * kk_ref[...]
    kk = kk / jnp.maximum(jnp.sqrt(_segsum(kk * kk, seg_ref[...])), 1e-12)
    kdir = k * (1.0 + (asig - 1.0) * ka_ref[...])
    bvec = kk * asig

    n_chunks = tt // CHUNK
    pack = lambda x: jnp.concatenate(
        [x[c * CHUNK:(c + 1) * CHUNK] for c in range(n_chunks)], axis=1)
    lw_p, r_p, kd_p, bv_p, kk_p, v_p = (pack(x) for x in (lw, r, kdir, bvec, kk, v))

    trow = lax.broadcasted_iota(jnp.int32, (CHUNK, CHUNK), 0)
    tcol = lax.broadcasted_iota(jnp.int32, (CHUNK, CHUNK), 1)
    tri = jnp.where((tcol >= trow) if rev else (tcol <= trow), 1.0, 0.0).astype(BF16)
    lw3 = _split3(lw_p)
    cum = _dot(tri, lw3[0]) + _dot(tri, lw3[1]) + _dot(tri, lw3[2])
    last = 0 if rev else CHUNK - 1
    tot = cum[last:last + 1, :]
    e_neg = jnp.exp(-cum)
    e_end = jnp.exp(tot - cum)
    r_t = r_p * jnp.exp(cum)
    k_t = kd_p * e_neg
    b_t = bv_p * e_neg
    a_t = -kk_p * jnp.exp(cum - lw_p)
    b_e = bv_p * e_end
    k_e = kd_p * e_end
    p_end = jnp.exp(tot)

    gw = 2 * LANES
    grow = lax.broadcasted_iota(jnp.int32, (gw, gw), 0)
    gcol = lax.broadcasted_iota(jnp.int32, (gw, gw), 1)
    bdmask = (grow // CHUNK) == (gcol // HEAD_DIM)
    tpos = lax.broadcasted_iota(jnp.int32, (CHUNK, gw), 0)
    spos = lax.broadcasted_iota(jnp.int32, (CHUNK, gw), 1) % CHUNK
    if rev:
        incl, strict = spos >= tpos, spos > tpos
    else:
        incl, strict = spos <= tpos, spos < tpos
    eye_g = jnp.where(spos == tpos, 1.0, 0.0)
    prow = lax.broadcasted_iota(jnp.int32, (LANES, LANES), 0)
    pcol = lax.broadcasted_iota(jnp.int32, (LANES, LANES), 1)
    pair_diag = (prow // HEAD_DIM) == (pcol // HEAD_DIM)

    def bd(xb):
        return jnp.where(bdmask, jnp.concatenate([xb] * 4, axis=0), jnp.zeros((), BF16))

    order = range(n_chunks - 1, -1, -1) if rev else range(n_chunks)

    def piece(x, p, c):
        lane0 = c * HEADS_W + p * LANES
        return x[:, lane0:lane0 + LANES]

    def grp(x, key):
        p, g = key
        return jnp.concatenate([piece(x, p, 2 * g), piece(x, p, 2 * g + 1)], axis=1)

    groups = [(p, g) for p in range(N_PAIRS) for g in range(n_chunks // 2)]
    pairs = range(N_PAIRS)
    rg = {k: grp(r_t, k) for k in groups}
    ag_b = {k: grp(a_t, k).astype(BF16) for k in groups}
    lhs = {k: jnp.concatenate([ag_b[k], rg[k].astype(BF16)], axis=0) for k in groups}
    s1 = {k: _dot_tb(lhs[k], bd(grp(b_t, k).astype(BF16))) for k in groups}
    s2 = {k: _dot_tb(lhs[k], bd(grp(k_t, k).astype(BF16))) for k in groups}
    a_ab = {k: jnp.where(strict, s1[k][:CHUNK], 0.0) for k in groups}
    a_rb = {k: jnp.where(incl, s1[k][CHUNK:], 0.0).astype(BF16) for k in groups}
    a_ak = {k: jnp.where(strict, s2[k][:CHUNK], 0.0).astype(BF16) for k in groups}
    a_rk = {k: jnp.where(incl, s2[k][CHUNK:], 0.0).astype(BF16) for k in groups}
    xb = {k: a_ab[k].astype(BF16) for k in groups}
    tinv = {k: eye_g + a_ab[k] for k in groups}
    xbd = {k: bd(xb[k]) for k in groups}
    for _ in range(5):
        xb = {k: _dot(xb[k], xbd[k]).astype(BF16) for k in groups}
        xbd = {k: bd(xb[k]) for k in groups}
        tinv = {k: tinv[k] + _dot(tinv[k].astype(BF16), xbd[k]) for k in groups}
    tb = {k: tinv[k].astype(BF16) for k in groups}
    vbd = {k: bd(grp(v_p, k).astype(BF16)) for k in groups}
    w2 = {k: _dot(a_ak[k], vbd[k]).astype(BF16) for k in groups}
    ta = {k: _dot(tb[k], bd(ag_b[k])) for k in groups}
    tv = {k: _dot(tb[k], bd(w2[k])) for k in groups}
    qg = {k: rg[k] + _dot(a_rb[k], bd(ta[k].astype(BF16))) for k in groups}
    yg = {k: _dot(a_rb[k], bd(tv[k].astype(BF16))) + _dot(a_rk[k], vbd[k]) for k in groups}

    def chunk_of(d, p, c):
        half = slice((c % 2) * LANES, (c % 2 + 1) * LANES)
        return d[(p, c // 2)][:, half]

    pcs = [(p, c) for c in order for p in pairs]
    be = {pc: piece(b_e, *pc) for pc in pcs}
    xt = {pc: jnp.where(pair_diag, _dot_ta(chunk_of(ta, *pc).astype(BF16),
                                           be[pc].astype(BF16)), 0.0).astype(BF16)
          for pc in pcs}
    ht = {pc: jnp.where(pair_diag, _dot_ta(
        jnp.concatenate([chunk_of(tv, *pc), piece(v_p, *pc)], axis=0).astype(BF16),
        jnp.concatenate([be[pc], piece(k_e, *pc)], axis=0).astype(BF16)), 0.0) for pc in pcs}

    state = {p: state_ref[p] for p in pairs}
    for ci in order:
        sb = {p: state[p].astype(BF16) for p in pairs}
        for p in pairs:
            y_ref[ci * CHUNK:(ci + 1) * CHUNK, p * LANES:(p + 1) * LANES] = (
                chunk_of(yg, p, ci) + _dot_tb(chunk_of(qg, p, ci).astype(BF16), sb[p]))
        state = {p: state[p] * piece(p_end, p, ci) + _dot(sb[p], xt[(p, ci)]) + ht[(p, ci)]
                 for p in pairs}
    for p in pairs:
        state_ref[p] = state[p]


def _scan(zc3, mu, w0, wdec, a0, wa, k_k, k_a, seg, b, t, tt, rev):
    nt = t // tt
    tile_of_step = (lambda s: nt - 1 - s) if rev else (lambda s: s)
    const = lambda i, s: (0, 0)
    vec = lambda w: pl.BlockSpec((1, w), const)
    in_specs = _halo_specs(RWKV_IN, tt, t, tile_of_step) + [
        vec(RWKV_IN), vec(HEADS_W), pl.BlockSpec((LORA_W, HEADS_W), const),
        vec(HEADS_W), pl.BlockSpec((LORA_W, HEADS_W), const),
        vec(HEADS_W), vec(HEADS_W), pl.BlockSpec((HEADS_W, HEADS_W), const)]
    return pl.pallas_call(
        functools.partial(_scan_kernel, tt=tt, nt=nt, rev=rev),
        grid=(b, nt),
        in_specs=in_specs,
        out_specs=pl.BlockSpec((None, tt, HEADS_W), lambda i, s: (i, tile_of_step(s), 0)),
        out_shape=jax.ShapeDtypeStruct((b, t, HEADS_W), F32),
        scratch_shapes=[pltpu.VMEM((N_PAIRS, LANES, LANES), F32)],
        compiler_params=pltpu.CompilerParams(
            dimension_semantics=("parallel", "arbitrary"), vmem_limit_bytes=VMEM_LIMIT),
        name="rwkv_scan_bwd" if rev else "rwkv_scan_fwd",
    )(zc3, zc3, zc3, mu, w0, wdec, a0, wa, k_k, k_a, seg)


def _post_kernel(zc_ref, chp_ref, chn_ref, zb_ref, bhp_ref, bhn_ref, yf_ref, yb_ref,
                 mu_ref, a0_ref, wa_ref, wg_ref, ka_ref, rk_ref, lnw_ref, lnb_ref,
                 cw_ref, seg_ref, yc_ref, conv_ref, *, nt):
    tau = pl.program_id(1)
    seg = seg_ref[...]

    zs = _token_shift(zc_ref, chp_ref, chn_ref, mu_ref, tau, nt)
    r = zs[:, 0:HEADS_W]
    k = zs[:, HEADS_W:2 * HEADS_W]
    v = zs[:, 2 * HEADS_W:3 * HEADS_W]
    lora = zs[:, 3 * HEADS_W:RWKV_IN]
    lora_b = lora.astype(BF16)
    asum = (_sigmoid(a0_ref[0:1, :] + _dot(lora_b, wa_ref[0]))
            + _sigmoid(a0_ref[1:2, :] + _dot(lora_b, wa_ref[1])))
    gate = _dot(_sigmoid(lora).astype(BF16), wg_ref[...])
    ksum = k * (2.0 + (asum - 2.0) * ka_ref[...])
    bonus = _segsum(r * ksum * rk_ref[...], seg) * v

    y = yf_ref[...] + yb_ref[...]
    mean = _segsum(y, seg) * (1.0 / HEAD_DIM)
    yc = y - mean
    var = _segsum(yc * yc, seg) * (1.0 / HEAD_DIM)
    yn = yc * lax.rsqrt(var + LNX_EPS) * lnw_ref[...] + lnb_ref[...]
    yc_ref[...] = (yn + bonus) * gate

    zb = zb_ref[...]
    prev_row, next_row = _halo_rows(bhp_ref, bhn_ref, tau, nt)
    gated = lambda z: z[:, CONV_W:2 * CONV_W] * z[:, 2 * CONV_W:3 * CONV_W]
    u_prev, u_next = _neighbours(gated(zb), gated(prev_row), gated(next_row))
    conv = u_prev * cw_ref[0:1, :] + gated(zb) * cw_ref[1:2, :] + u_next * cw_ref[2:3, :]
    conv_ref[...] = zb[:, 0:CONV_W] * conv


def _post(zc3, zb3, yf, yb, mu, a0, wa, wg, k_a, r_k, lnw, lnb, cw, seg, b, t, tt):
    nt = t // tt
    ident = lambda s: s
    const2 = lambda i, s: (0, 0)
    const3 = lambda i, s: (0, 0, 0)
    vec = lambda w: pl.BlockSpec((1, w), const2)
    tile = lambda w: pl.BlockSpec((None, tt, w), lambda i, s: (i, s, 0))
    in_specs = (_halo_specs(RWKV_IN, tt, t, ident) + _halo_specs(CONVIN_W, tt, t, ident) + [
        tile(HEADS_W), tile(HEADS_W), vec(RWKV_IN),
        pl.BlockSpec((2, HEADS_W), const2), pl.BlockSpec((2, LORA_W, HEADS_W), const3),
        pl.BlockSpec((LORA_W, HEADS_W), const2), vec(HEADS_W), vec(HEADS_W), vec(HEADS_W),
        vec(HEADS_W), pl.BlockSpec((3, CONV_W), const2),
        pl.BlockSpec((HEADS_W, HEADS_W), const2)])
    return pl.pallas_call(
        functools.partial(_post_kernel, nt=nt),
        grid=(b, nt),
        in_specs=in_specs,
        out_specs=[tile(HEADS_W), tile(CONV_W)],
        out_shape=[jax.ShapeDtypeStruct((b, t, HEADS_W), F32),
                   jax.ShapeDtypeStruct((b, t, CONV_W), F32)],
        compiler_params=pltpu.CompilerParams(
            dimension_semantics=("parallel", "parallel"), vmem_limit_bytes=VMEM_LIMIT),
        name="rwkv_post_conv",
    )(zc3, zc3, zc3, zb3, zb3, zb3, yf, yb, mu, a0, wa, wg, k_a, r_k, lnw, lnb, cw, seg)


def _mixout_kernel(x_ref, att_ref, conv_ref, yc_ref, ga_ref, gb_ref, w_ref, gpost_ref, o_ref):
    ya = _rms(att_ref[...], ga_ref[...]).astype(BF16)
    yb = _rms(conv_ref[...], gb_ref[...]).astype(BF16)
    yc = yc_ref[...].astype(BF16)
    mix = (_dot(ya, w_ref[0:HEADS_W, :])
           + _dot(yb, w_ref[HEADS_W:HEADS_W + CONV_W, :])
           + _dot(yc, w_ref[HEADS_W + CONV_W:, :]))
    o_ref[...] = x_ref[...] + _rms(mix, gpost_ref[...])


def _mixout(x2, att, conv, yc, ga, gb, w_bf16, gpost, tm):
    n = x2.shape[0]
    row = lambda i: (i, 0)
    const = lambda i: (0, 0)
    return pl.pallas_call(
        _mixout_kernel,
        grid=(n // tm,),
        in_specs=[pl.BlockSpec((tm, D_MODEL), row), pl.BlockSpec((tm, HEADS_W), row),
                  pl.BlockSpec((tm, CONV_W), row), pl.BlockSpec((tm, HEADS_W), row),
                  pl.BlockSpec((1, HEADS_W), const), pl.BlockSpec((1, CONV_W), const),
                  pl.BlockSpec((D_MODEL, D_MODEL), const), pl.BlockSpec((1, D_MODEL), const)],
        out_specs=pl.BlockSpec((tm, D_MODEL), row),
        out_shape=jax.ShapeDtypeStruct((n, D_MODEL), F32),
        compiler_params=pltpu.CompilerParams(
            dimension_semantics=("parallel",), vmem_limit_bytes=VMEM_LIMIT),
        name="mix_out",
    )(x2, att, conv, yc, ga, gb, w_bf16, gpost)


def _ffn_kernel(x_ref, gpre_ref, w1_ref, w2_ref, gpost_ref, o_ref, *, ff_tile):
    x = x_ref[...]
    h = _rms(x, gpre_ref[...]).astype(BF16)
    acc = None
    for j in range(D_FF // ff_tile):
        sl = slice(j * ff_tile, (j + 1) * ff_tile)
        a = jnp.maximum(_dot(h, w1_ref[:, sl]), 0.0)
        part = _dot((a * a).astype(BF16), w2_ref[sl, :])
        acc = part if acc is None else acc + part
    o_ref[...] = x + _rms(acc, gpost_ref[...])


def _ffn(x2, gpre, w1_bf16, w2_bf16, gpost, tm):
    n = x2.shape[0]
    row = lambda i: (i, 0)
    const = lambda i: (0, 0)
    return pl.pallas_call(
        functools.partial(_ffn_kernel, ff_tile=1024),
        grid=(n // tm,),
        in_specs=[pl.BlockSpec((tm, D_MODEL), row), pl.BlockSpec((1, D_MODEL), const),
                  pl.BlockSpec((D_MODEL, D_FF), const), pl.BlockSpec((D_FF, D_MODEL), const),
                  pl.BlockSpec((1, D_MODEL), const)],
        out_specs=pl.BlockSpec((tm, D_MODEL), row),
        out_shape=jax.ShapeDtypeStruct((n, D_MODEL), F32),
        compiler_params=pltpu.CompilerParams(
            dimension_semantics=("parallel",), vmem_limit_bytes=VMEM_LIMIT),
        name="ffn",
    )(x2, gpre, w1_bf16, w2_bf16, gpost)


def _pad_rows(w, start):
    return jnp.zeros((LORA_W, HEADS_W), F32).at[start:start + w.shape[0]].set(w)


def _layer_params(l, p):
    row = lambda a: a.reshape(1, -1).astype(F32)
    seg = np.kron(np.eye(N_HEADS, dtype=np.float32), np.ones((HEAD_DIM, HEAD_DIM), np.float32))
    return dict(
        g_mix_pre=row(p["norm_mix_pre"][l]), g_mix_post=row(p["norm_mix_post"][l]),
        g_ffn_pre=row(p["norm_ffn_pre"][l]), g_ffn_post=row(p["norm_ffn_post"][l]),
        w_in=p["w_in"][l].astype(BF16), w_out=p["w_out"][l].astype(BF16),
        attn_g=row(p["attn_out_g"][l]), conv_w=p["conv_w"][l].astype(F32),
        conv_g=row(p["conv_out_g"][l]), mu=row(p["rwkv_mu"][l]),
        w0=p["decay_w0"][l].astype(F32), a0=p["iclr_a0"][l].astype(F32),
        wdec=jnp.stack([_pad_rows(p["decay_up"][l, d], 0) for d in range(2)]).astype(BF16),
        wa=jnp.stack([_pad_rows(p["iclr_up"][l, d], DECAY_RANK) for d in range(2)]).astype(BF16),
        wg=_pad_rows(p["gate_up"][l], DECAY_RANK + ICLR_RANK).astype(BF16),
        k_k=row(p["k_k"][l]), k_a=row(p["k_a"][l]), r_k=row(p["r_k"][l]),
        lnx_w=row(p["lnx_w"][l]), lnx_b=row(p["lnx_b"][l]),
        w1=p["ffn_w1"][l].astype(BF16), w2=p["ffn_w2"][l].astype(BF16),
        seg=jnp.asarray(seg, BF16),
    )


def _layer(x2, lp, biases, b, t):
    n = b * t
    tm = min(512, n)
    tt = min(256, t)
    zq, zb, zc = _inproj(x2, lp["g_mix_pre"], lp["w_in"], tm)
    att = _attention(zq, biases, b, t)
    zc3 = zc.reshape(b, t, RWKV_IN)
    zb3 = zb.reshape(b, t, CONVIN_W)
    ys = [_scan(zc3, lp["mu"], lp["w0"][d:d + 1], lp["wdec"][d], lp["a0"][d:d + 1], lp["wa"][d],
                lp["k_k"], lp["k_a"], lp["seg"], b, t, tt, rev=bool(d)) for d in range(2)]
    yc, conv = _post(zc3, zb3, ys[0], ys[1], lp["mu"], lp["a0"], lp["wa"], lp["wg"], lp["k_a"],
                     lp["r_k"], lp["lnx_w"], lp["lnx_b"], lp["conv_w"], lp["seg"], b, t, tt)
    x2 = _mixout(x2, att, conv.reshape(n, CONV_W), yc.reshape(n, HEADS_W), lp["attn_g"],
                 lp["conv_g"], lp["w_out"], lp["g_mix_post"], tm)
    return _ffn(x2, lp["g_ffn_pre"], lp["w1"], lp["w2"], lp["g_ffn_post"], tm)


def _trunk(x, rel_bias, layers):
    b, t, _ = x.shape
    biases = [_attn_bias(rel_bias, t, dil) for dil in DILATIONS]
    x2 = x.reshape(b * t, D_MODEL)
    for lp in layers:
        x2 = _layer(x2, lp, biases, b, t)
    return x2.reshape(b, t, D_MODEL)


def kernel(x_prompt, x_sample, rel_bias, norm_mix_pre, norm_mix_post, norm_ffn_pre, norm_ffn_post, w_in, w_out, attn_out_g, conv_w, conv_out_g, rwkv_mu, decay_w0, decay_up, iclr_a0, iclr_up, gate_up, k_k, k_a, r_k, lnx_w, lnx_b, ffn_w1, ffn_w2):
    p = dict(norm_mix_pre=norm_mix_pre, norm_mix_post=norm_mix_post, norm_ffn_pre=norm_ffn_pre,
             norm_ffn_post=norm_ffn_post, w_in=w_in, w_out=w_out, attn_out_g=attn_out_g,
             conv_w=conv_w, conv_out_g=conv_out_g, rwkv_mu=rwkv_mu, decay_w0=decay_w0,
             decay_up=decay_up, iclr_a0=iclr_a0, iclr_up=iclr_up, gate_up=gate_up, k_k=k_k,
             k_a=k_a, r_k=r_k, lnx_w=lnx_w, lnx_b=lnx_b, ffn_w1=ffn_w1, ffn_w2=ffn_w2)
    layers = [_layer_params(l, p) for l in range(w_in.shape[0])]
    return (_trunk(x_prompt, rel_bias, layers), _trunk(x_sample, rel_bias, layers))
```

```python
import functools
import math

import numpy as np
import jax
import jax.numpy as jnp
from jax import lax
from jax.experimental import pallas as pl
from jax.experimental.pallas import tpu as pltpu

F32 = jnp.float32
BF16 = jnp.bfloat16

D_MODEL = 1024
HEAD_DIM = 64
N_HEADS = 6
HEADS_W = N_HEADS * HEAD_DIM
CONV_W = 256
QKV_W = 3 * HEADS_W
CONVIN_W = 3 * CONV_W
LORA_W = 128
DECAY_RANK = 32
ICLR_RANK = 32
RWKV_IN = 3 * HEADS_W + LORA_W
IN_WIDTH = QKV_W + CONVIN_W + RWKV_IN
D_FF = 4 * D_MODEL
DILATIONS = (1, 4, 16)
KEYS_PER_SIDE = 64
ATTN_UNROLL = 4
N_BUCKETS = 32
BUCKET_MAX_DIST = 1024
RMS_EPS = 1e-6
LNX_EPS = 64e-5
CHUNK = 64
LANES = 128
N_PAIRS = HEADS_W // LANES
HALO_ROWS = 8
VMEM_LIMIT = 56 * 1024 * 1024


def _rms(x, g, eps=RMS_EPS):
    return x * lax.rsqrt(jnp.mean(x * x, axis=-1, keepdims=True) + eps) * g


def _sigmoid(x):
    return 1.0 / (1.0 + jnp.exp(-x))


def _dot(a, b):
    return jnp.dot(a, b, preferred_element_type=F32)


def _dot_tb(a, b):
    return lax.dot_general(a, b, (((1,), (1,)), ((), ())), preferred_element_type=F32)


def _dot_ta(a, b):
    return lax.dot_general(a, b, (((0,), (0,)), ((), ())), preferred_element_type=F32)


def _split2(x):
    hi = x.astype(BF16)
    lo = (x - hi.astype(F32)).astype(BF16)
    return hi, lo


def _split3(x):
    hi = x.astype(BF16)
    r1 = x - hi.astype(F32)
    mid = r1.astype(BF16)
    lo = (r1 - mid.astype(F32)).astype(BF16)
    return hi, mid, lo


def _segsum(x, seg_ones):
    hi, lo = _split2(x)
    return _dot(hi, seg_ones) + _dot(lo, seg_ones)


def _inproj_kernel(x_ref, g_ref, w_ref, zq_ref, zb_ref, zc_ref):
    h = _rms(x_ref[...], g_ref[...]).astype(BF16)
    zq_ref[...] = _dot(h, w_ref[:, 0:QKV_W])
    zb_ref[...] = _dot(h, w_ref[:, QKV_W:QKV_W + CONVIN_W])
    zc_ref[...] = _dot(h, w_ref[:, QKV_W + CONVIN_W:IN_WIDTH])


def _inproj(x2, g, w_bf16, tm):
    n = x2.shape[0]
    row = lambda i: (i, 0)
    const = lambda i: (0, 0)
    return pl.pallas_call(
        _inproj_kernel,
        grid=(n // tm,),
        in_specs=[pl.BlockSpec((tm, D_MODEL), row),
                  pl.BlockSpec((1, D_MODEL), const),
                  pl.BlockSpec((D_MODEL, IN_WIDTH), const)],
        out_specs=[pl.BlockSpec((tm, QKV_W), row),
                   pl.BlockSpec((tm, CONVIN_W), row),
                   pl.BlockSpec((tm, RWKV_IN), row)],
        out_shape=[jax.ShapeDtypeStruct((n, QKV_W), F32),
                   jax.ShapeDtypeStruct((n, CONVIN_W), F32),
                   jax.ShapeDtypeStruct((n, RWKV_IN), F32)],
        compiler_params=pltpu.CompilerParams(
            dimension_semantics=("parallel",), vmem_limit_bytes=VMEM_LIMIT),
        name="inproj",
    )(x2, g, w_bf16)


def _t5_bucket(rel):
    half = N_BUCKETS // 2
    max_exact = half // 2
    ret = np.where(rel > 0, half, 0)
    n = np.abs(rel)
    large = max_exact + (np.log(np.maximum(n, 1) / max_exact)
                         / np.log(BUCKET_MAX_DIST / max_exact) * (half - max_exact)).astype(np.int32)
    large = np.minimum(large, half - 1)
    return (ret + np.where(n < max_exact, n, large)).astype(np.int32)


def _attn_cfg(t, dil):
    cls_len = t // dil
    bq = min(128, cls_len)
    wk = min(bq + 2 * KEYS_PER_SIDE, cls_len)
    return cls_len, bq, wk, cls_len // bq


def _attn_bias(rel_bias, t, dil):
    _, bq, wk, _ = _attn_cfg(t, dil)
    out = []
    for shift in (0, KEYS_PER_SIDE, wk - bq):
        rel = np.arange(wk)[None, :] - np.arange(bq)[:, None] - shift
        valid = np.abs(rel) <= KEYS_PER_SIDE
        onehot = jnp.asarray(_t5_bucket(rel * dil))[:, :, None] == jnp.arange(N_BUCKETS)
        table = jnp.transpose(rel_bias.astype(F32))[:, None, None, :]
        bias = jnp.sum(jnp.where(onehot[None], table, 0.0), axis=-1)
        out.append(jnp.where(valid[None], bias, -jnp.inf))
    return jnp.stack(out)


def _attn_kernel(q_ref, k_ref, v_ref, b0_ref, b1_ref, b2_ref, o_ref, m_ref, s_ref, *, t):
    lane = lax.broadcasted_iota(jnp.int32, (1, LANES), 1)
    head0 = lane < HEAD_DIM
    bias_refs = (b0_ref, b1_ref, b2_ref)

    for di, dil in enumerate(DILATIONS):
        cls_len, bq, wk, nb = _attn_cfg(t, dil)
        bias_ref = bias_refs[di]

        def rows(start, size, dil=dil):
            if dil == 1:
                return pl.ds(start, size)
            return pl.ds(start, size, stride=dil)

        def do_blocks(blocks, di=di, dil=dil, bq=bq, wk=wk, cls_len=cls_len,
                      bias_ref=bias_ref, rows=rows):
            qrows, q, kw, vw = [], [], [], []
            for c, blk, placement in blocks:
                m0 = blk * bq
                ws = (0, m0 - KEYS_PER_SIDE, cls_len - wk)[placement]
                qrows.append(rows(c + dil * m0, bq))
                krows = rows(c + dil * ws, wk)
                q.append(q_ref[qrows[-1], :] * (HEAD_DIM ** -0.5))
                kw.append(k_ref[krows, :].astype(BF16))
                vw.append(v_ref[krows, :].astype(BF16))
            items = [(i, h) for i in range(len(blocks)) for h in range(2)]
            hmask = (head0, jnp.logical_not(head0))
            logits = {(i, h): _dot_tb(jnp.where(hmask[h], q[i], 0.0).astype(BF16), kw[i])
                      + bias_ref[blocks[i][2], h] for i, h in items}
            mh = {k: jnp.max(logits[k], axis=-1, keepdims=True) for k in items}
            p = {k: jnp.exp(logits[k] - mh[k]) for k in items}
            sh = {k: jnp.sum(p[k], axis=-1, keepdims=True) for k in items}
            oh = {(i, h): _dot(p[(i, h)].astype(BF16), vw[i]) for i, h in items}
            for i in range(len(blocks)):
                m_blk = jnp.where(head0, mh[(i, 0)], mh[(i, 1)])
                s_blk = jnp.where(head0, sh[(i, 0)], sh[(i, 1)])
                o_blk = jnp.where(head0, oh[(i, 0)], oh[(i, 1)])
                if di == 0:
                    m_ref[qrows[i], :] = m_blk
                    s_ref[qrows[i], :] = s_blk
                    o_ref[qrows[i], :] = o_blk
                else:
                    m_old = m_ref[qrows[i], :]
                    m_new = jnp.maximum(m_old, m_blk)
                    a_old = jnp.exp(m_old - m_new)
                    a_blk = jnp.exp(m_blk - m_new)
                    m_ref[qrows[i], :] = m_new
                    s_ref[qrows[i], :] = s_ref[qrows[i], :] * a_old + s_blk * a_blk
                    o_ref[qrows[i], :] = o_ref[qrows[i], :] * a_old + o_blk * a_blk

        def loop(n, body):
            if n == 1:
                body(0)
            else:
                lax.fori_loop(0, n, lambda i, carry: (body(i), carry)[1], 0)

        edge = [(0, 0)] + ([(nb - 1, 2)] if nb > 1 else [])
        cb = math.gcd(dil, max(1, ATTN_UNROLL // len(edge)))
        loop(dil // cb, lambda i, edge=edge, cb=cb, do_blocks=do_blocks: do_blocks(
            [(i * cb + j, blk, placement) for j in range(cb) for blk, placement in edge]))
        n_mid = max(nb - 2, 0)
        if n_mid:
            u = max(d for d in range(1, ATTN_UNROLL + 1) if n_mid % d == 0)
            per = n_mid // u
            loop(dil * per, lambda i, u=u, per=per, do_blocks=do_blocks: do_blocks(
                [(i // per, 1 + (i % per) * u + j, 1) for j in range(u)]))

    rb = min(256, t)

    def norm_body(i, carry):
        sl = pl.ds(pl.multiple_of(i * rb, rb), rb)
        o_ref[sl, :] = o_ref[sl, :] / s_ref[sl, :]
        return carry

    lax.fori_loop(0, t // rb, norm_body, 0)


def _attention(zq, biases, b, t):
    zq3 = zq.reshape(b, t, QKV_W)
    in_specs = [pl.BlockSpec((None, t, LANES), lambda i, hp: (i, 0, hp)),
                pl.BlockSpec((None, t, LANES), lambda i, hp: (i, 0, N_PAIRS + hp)),
                pl.BlockSpec((None, t, LANES), lambda i, hp: (i, 0, 2 * N_PAIRS + hp))]
    for bias in biases:
        in_specs.append(pl.BlockSpec((3, 2) + bias.shape[2:], lambda i, hp: (0, hp, 0, 0)))
    out = pl.pallas_call(
        functools.partial(_attn_kernel, t=t),
        grid=(b, N_PAIRS),
        in_specs=in_specs,
        out_specs=pl.BlockSpec((None, t, LANES), lambda i, hp: (i, 0, hp)),
        out_shape=jax.ShapeDtypeStruct((b, t, HEADS_W), F32),
        scratch_shapes=[pltpu.VMEM((t, LANES), F32), pltpu.VMEM((t, LANES), F32)],
        compiler_params=pltpu.CompilerParams(
            dimension_semantics=("parallel", "parallel"), vmem_limit_bytes=VMEM_LIMIT),
        name="dilated_attention",
    )(zq3, zq3, zq3, *biases)
    return out.reshape(b * t, HEADS_W)


def _neighbours(main, prev_row, next_row):
    tt = main.shape[0]
    row = lax.broadcasted_iota(jnp.int32, main.shape, 0)
    prev = jnp.where(row == 0, prev_row, pltpu.roll(main, 1, 0))
    nxt = jnp.where(row == tt - 1, next_row, pltpu.roll(main, tt - 1, 0))
    return prev, nxt


def _halo_rows(hp_ref, hn_ref, tau, nt):
    prev_row = jnp.where(tau > 0, hp_ref[HALO_ROWS - 1:HALO_ROWS, :], 0.0)
    next_row = jnp.where(tau < nt - 1, hn_ref[0:1, :], 0.0)
    return prev_row, next_row


def _token_shift(zc_ref, hp_ref, hn_ref, mu_ref, tau, nt):
    zc = zc_ref[...]
    prev_row, next_row = _halo_rows(hp_ref, hn_ref, tau, nt)
    prev, nxt = _neighbours(zc, prev_row, next_row)
    return zc + (0.5 * (prev + nxt) - zc) * mu_ref[...]


def _halo_specs(width, tt, t, tile_of_step):
    per = tt // HALO_ROWS
    last = t // HALO_ROWS - 1
    main = pl.BlockSpec((None, tt, width), lambda i, s: (i, tile_of_step(s), 0))
    prev = pl.BlockSpec((None, HALO_ROWS, width),
                        lambda i, s: (i, jnp.maximum(tile_of_step(s) * per - 1, 0), 0))
    nxt = pl.BlockSpec((None, HALO_ROWS, width),
                       lambda i, s: (i, jnp.minimum((tile_of_step(s) + 1) * per, last), 0))
    return [main, prev, nxt]


def _scan_kernel(zc_ref, hp_ref, hn_ref, mu_ref, w0_ref, wdec_ref, a0_ref, wa_ref,
                 kk_ref, ka_ref, seg_ref, y_ref, state_ref, *, tt, nt, rev):
    step = pl.program_id(1)
    tau = (nt - 1 - step) if rev else step

    @pl.when(step == 0)
    def _():
        state_ref[...] = jnp.zeros_like(state_ref)

    zs = _token_shift(zc_ref, hp_ref, hn_ref, mu_ref, tau, nt)
    r = zs[:, 0:HEADS_W]
    k = zs[:, HEADS_W:2 * HEADS_W]
    v = zs[:, 2 * HEADS_W:3 * HEADS_W]
    lora = zs[:, 3 * HEADS_W:RWKV_IN]

    xdec = w0_ref[...] + _dot(jnp.tanh(lora).astype(BF16), wdec_ref[...])
    lw = (-math.exp(-0.5)) * _sigmoid(xdec)
    asig = _sigmoid(a0_ref[...] + _dot(lora.astype(BF16), wa_ref[...]))
    kk = k * kk_ref[...]
    kk = kk / jnp.maximum(jnp.sqrt(_segsum(kk * kk, seg_ref[...])), 1e-12)
    kdir = k * (1.0 + (asig - 1.0) * ka_ref[...])
    bvec = kk * asig

    n_chunks = tt // CHUNK
    pack = lambda x: jnp.concatenate(
        [x[c * CHUNK:(c + 1) * CHUNK] for c in range(n_chunks)], axis=1)
    lw_p, r_p, kd_p, bv_p, kk_p, v_p = (pack(x) for x in (lw, r, kdir, bvec, kk, v))

    trow = lax.broadcasted_iota(jnp.int32, (CHUNK, CHUNK), 0)
    tcol = lax.broadcasted_iota(jnp.int32, (CHUNK, CHUNK), 1)
    tri = jnp.where((tcol >= trow) if rev else (tcol <= trow), 1.0, 0.0).astype(BF16)
    lw3 = _split3(lw_p)
    cum = _dot(tri, lw3[0]) + _dot(tri, lw3[1]) + _dot(tri, lw3[2])
    last = 0 if rev else CHUNK - 1
    tot = cum[last:last + 1, :]
    e_neg = jnp.exp(-cum)
    e_end = jnp.exp(tot - cum)
    r_t = r_p * jnp.exp(cum)
    k_t = kd_p * e_neg
    b_t = bv_p * e_neg
    a_t = -kk_p * jnp.exp(cum - lw_p)
    b_e = bv_p * e_end
    k_e = kd_p * e_end
    p_end = jnp.exp(tot)

    gw = 2 * LANES
    grow = lax.broadcasted_iota(jnp.int32, (gw, gw), 0)
    gcol = lax.broadcasted_iota(jnp.int32, (gw, gw), 1)
    bdmask = (grow // CHUNK) == (gcol // HEAD_DIM)
    tpos = lax.broadcasted_iota(jnp.int32, (CHUNK, gw), 0)
    spos = lax.broadcasted_iota(jnp.int32, (CHUNK, gw), 1) % CHUNK
    if rev:
        incl, strict = spos >= tpos, spos > tpos
    else:
        incl, strict = spos <= tpos, spos < tpos
    eye_g = jnp.where(spos == tpos, 1.0, 0.0)
    prow = lax.broadcasted_iota(jnp.int32, (LANES, LANES), 0)
    pcol = lax.broadcasted_iota(jnp.int32, (LANES, LANES), 1)
    pair_diag = (prow // HEAD_DIM) == (pcol // HEAD_DIM)

    def bd(xb):
        return jnp.where(bdmask, jnp.concatenate([xb] * 4, axis=0), jnp.zeros((), BF16))

    order = range(n_chunks - 1, -1, -1) if rev else range(n_chunks)

    def piece(x, p, c):
        lane0 = c * HEADS_W + p * LANES
        return x[:, lane0:lane0 + LANES]

    def grp(x, key):
        p, g = key
        return jnp.concatenate([piece(x, p, 2 * g), piece(x, p, 2 * g + 1)], axis=1)

    groups = [(p, g) for p in range(N_PAIRS) for g in range(n_chunks // 2)]
    pairs = range(N_PAIRS)
    rg = {k: grp(r_t, k) for k in groups}
    ag_b = {k: grp(a_t, k).astype(BF16) for k in groups}
    lhs = {k: jnp.concatenate([ag_b[k], rg[k].astype(BF16)], axis=0) for k in groups}
    s1 = {k: _dot_tb(lhs[k], bd(grp(b_t, k).astype(BF16))) for k in groups}
    s2 = {k: _dot_tb(lhs[k], bd(grp(k_t, k).astype(BF16))) for k in groups}
    a_ab = {k: jnp.where(strict, s1[k][:CHUNK], 0.0) for k in groups}
    a_rb = {k: jnp.where(incl, s1[k][CHUNK:], 0.0).astype(BF16) for k in groups}
    a_ak = {k: jnp.where(strict, s2[k][:CHUNK], 0.0).astype(BF16) for k in groups}
    a_rk = {k: jnp.where(incl, s2[k][CHUNK:], 0.0).astype(BF16) for k in groups}
    def sibling(m):
        return jnp.logical_and(tpos // (2 * m) == spos // (2 * m), tpos // m != spos // m)

    tinv = {k: eye_g + jnp.where(sibling(1), a_ab[k], 0.0) for k in groups}
    m = 2
    while m < CHUNK:
        a_off = {k: jnp.where(sibling(m), a_ab[k], 0.0).astype(BF16) for k in groups}
        db = {k: tinv[k].astype(BF16) for k in groups}
        yb = {k: _dot(a_off[k], bd(db[k])).astype(BF16) for k in groups}
        tinv = {k: tinv[k] + _dot(db[k], bd(yb[k])) for k in groups}
        m *= 2
    tb = {k: tinv[k].astype(BF16) for k in groups}
    vbd = {k: bd(grp(v_p, k).astype(BF16)) for k in groups}
    w2 = {k: _dot(a_ak[k], vbd[k]).astype(BF16) for k in groups}
    ta = {k: _dot(tb[k], bd(ag_b[k])) for k in groups}
    tv = {k: _dot(tb[k], bd(w2[k])) for k in groups}
    qg = {k: rg[k] + _dot(a_rb[k], bd(ta[k].astype(BF16))) for k in groups}
    yg = {k: _dot(a_rb[k], bd(tv[k].astype(BF16))) + _dot(a_rk[k], vbd[k]) for k in groups}

    def chunk_of(d, p, c):
        half = slice((c % 2) * LANES, (c % 2 + 1) * LANES)
        return d[(p, c // 2)][:, half]

    pcs = [(p, c) for c in order for p in pairs]
    be = {pc: piece(b_e, *pc) for pc in pcs}
    xt = {pc: jnp.where(pair_diag, _dot_ta(chunk_of(ta, *pc).astype(BF16),
                                           be[pc].astype(BF16)), 0.0).astype(BF16)
          for pc in pcs}
    ht = {pc: jnp.where(pair_diag, _dot_ta(
        jnp.concatenate([chunk_of(tv, *pc), piece(v_p, *pc)], axis=0).astype(BF16),
        jnp.concatenate([be[pc], piece(k_e, *pc)], axis=0).astype(BF16)), 0.0) for pc in pcs}

    state = {p: state_ref[p] for p in pairs}
    for ci in order:
        sb = {p: state[p].astype(BF16) for p in pairs}
        for p in pairs:
            y_ref[ci * CHUNK:(ci + 1) * CHUNK, p * LANES:(p + 1) * LANES] = (
                chunk_of(yg, p, ci) + _dot_tb(chunk_of(qg, p, ci).astype(BF16), sb[p]))
        state = {p: state[p] * piece(p_end, p, ci) + _dot(sb[p], xt[(p, ci)]) + ht[(p, ci)]
                 for p in pairs}
    for p in pairs:
        state_ref[p] = state[p]


def _scan(zc3, mu, w0, wdec, a0, wa, k_k, k_a, seg, b, t, tt, rev):
    nt = t // tt
    tile_of_step = (lambda s: nt - 1 - s) if rev else (lambda s: s)
    const = lambda i, s: (0, 0)
    vec = lambda w: pl.BlockSpec((1, w), const)
    in_specs = _halo_specs(RWKV_IN, tt, t, tile_of_step) + [
        vec(RWKV_IN), vec(HEADS_W), pl.BlockSpec((LORA_W, HEADS_W), const),
        vec(HEADS_W), pl.BlockSpec((LORA_W, HEADS_W), const),
        vec(HEADS_W), vec(HEADS_W), pl.BlockSpec((HEADS_W, HEADS_W), const)]
    return pl.pallas_call(
        functools.partial(_scan_kernel, tt=tt, nt=nt, rev=rev),
        grid=(b, nt),
        in_specs=in_specs,
        out_specs=pl.BlockSpec((None, tt, HEADS_W), lambda i, s: (i, tile_of_step(s), 0)),
        out_shape=jax.ShapeDtypeStruct((b, t, HEADS_W), F32),
        scratch_shapes=[pltpu.VMEM((N_PAIRS, LANES, LANES), F32)],
        compiler_params=pltpu.CompilerParams(
            dimension_semantics=("parallel", "arbitrary"), vmem_limit_bytes=VMEM_LIMIT),
        name="rwkv_scan_bwd" if rev else "rwkv_scan_fwd",
    )(zc3, zc3, zc3, mu, w0, wdec, a0, wa, k_k, k_a, seg)


def _post_kernel(zc_ref, chp_ref, chn_ref, zb_ref, bhp_ref, bhn_ref, yf_ref, yb_ref,
                 mu_ref, a0_ref, wa_ref, wg_ref, ka_ref, rk_ref, lnw_ref, lnb_ref,
                 cw_ref, seg_ref, yc_ref, conv_ref, *, nt):
    tau = pl.program_id(1)
    seg = seg_ref[...]

    zs = _token_shift(zc_ref, chp_ref, chn_ref, mu_ref, tau, nt)
    r = zs[:, 0:HEADS_W]
    k = zs[:, HEADS_W:2 * HEADS_W]
    v = zs[:, 2 * HEADS_W:3 * HEADS_W]
    lora = zs[:, 3 * HEADS_W:RWKV_IN]
    lora_b = lora.astype(BF16)
    asum = (_sigmoid(a0_ref[0:1, :] + _dot(lora_b, wa_ref[0]))
            + _sigmoid(a0_ref[1:2, :] + _dot(lora_b, wa_ref[1])))
    gate = _dot(_sigmoid(lora).astype(BF16), wg_ref[...])
    ksum = k * (2.0 + (asum - 2.0) * ka_ref[...])
    bonus = _segsum(r * ksum * rk_ref[...], seg) * v

    y = yf_ref[...] + yb_ref[...]
    mean = _segsum(y, seg) * (1.0 / HEAD_DIM)
    yc = y - mean
    var = _segsum(yc * yc, seg) * (1.0 / HEAD_DIM)
    yn = yc * lax.rsqrt(var + LNX_EPS) * lnw_ref[...] + lnb_ref[...]
    yc_ref[...] = (yn + bonus) * gate

    zb = zb_ref[...]
    prev_row, next_row = _halo_rows(bhp_ref, bhn_ref, tau, nt)
    gated = lambda z: z[:, CONV_W:2 * CONV_W] * z[:, 2 * CONV_W:3 * CONV_W]
    u_prev, u_next = _neighbours(gated(zb), gated(prev_row), gated(next_row))
    conv = u_prev * cw_ref[0:1, :] + gated(zb) * cw_ref[1:2, :] + u_next * cw_ref[2:3, :]
    conv_ref[...] = zb[:, 0:CONV_W] * conv


def _post(zc3, zb3, yf, yb, mu, a0, wa, wg, k_a, r_k, lnw, lnb, cw, seg, b, t, tt):
    nt = t // tt
    ident = lambda s: s
    const2 = lambda i, s: (0, 0)
    const3 = lambda i, s: (0, 0, 0)
    vec = lambda w: pl.BlockSpec((1, w), const2)
    tile = lambda w: pl.BlockSpec((None, tt, w), lambda i, s: (i, s, 0))
    in_specs = (_halo_specs(RWKV_IN, tt, t, ident) + _halo_specs(CONVIN_W, tt, t, ident) + [
        tile(HEADS_W), tile(HEADS_W), vec(RWKV_IN),
        pl.BlockSpec((2, HEADS_W), const2), pl.BlockSpec((2, LORA_W, HEADS_W), const3),
        pl.BlockSpec((LORA_W, HEADS_W), const2), vec(HEADS_W), vec(HEADS_W), vec(HEADS_W),
        vec(HEADS_W), pl.BlockSpec((3, CONV_W), const2),
        pl.BlockSpec((HEADS_W, HEADS_W), const2)])
    return pl.pallas_call(
        functools.partial(_post_kernel, nt=nt),
        grid=(b, nt),
        in_specs=in_specs,
        out_specs=[tile(HEADS_W), tile(CONV_W)],
        out_shape=[jax.ShapeDtypeStruct((b, t, HEADS_W), F32),
                   jax.ShapeDtypeStruct((b, t, CONV_W), F32)],
        compiler_params=pltpu.CompilerParams(
            dimension_semantics=("parallel", "parallel"), vmem_limit_bytes=VMEM_LIMIT),
        name="rwkv_post_conv",
    )(zc3, zc3, zc3, zb3, zb3, zb3, yf, yb, mu, a0, wa, wg, k_a, r_k, lnw, lnb, cw, seg)


def _mixout_kernel(x_ref, att_ref, conv_ref, yc_ref, ga_ref, gb_ref, w_ref, gpost_ref, o_ref):
    ya = _rms(att_ref[...], ga_ref[...]).astype(BF16)
    yb = _rms(conv_ref[...], gb_ref[...]).astype(BF16)
    yc = yc_ref[...].astype(BF16)
    mix = (_dot(ya, w_ref[0:HEADS_W, :])
           + _dot(yb, w_ref[HEADS_W:HEADS_W + CONV_W, :])
           + _dot(yc, w_ref[HEADS_W + CONV_W:, :]))
    o_ref[...] = x_ref[...] + _rms(mix, gpost_ref[...])


def _mixout(x2, att, conv, yc, ga, gb, w_bf16, gpost, tm):
    n = x2.shape[0]
    row = lambda i: (i, 0)
    const = lambda i: (0, 0)
    return pl.pallas_call(
        _mixout_kernel,
        grid=(n // tm,),
        in_specs=[pl.BlockSpec((tm, D_MODEL), row), pl.BlockSpec((tm, HEADS_W), row),
                  pl.BlockSpec((tm, CONV_W), row), pl.BlockSpec((tm, HEADS_W), row),
                  pl.BlockSpec((1, HEADS_W), const), pl.BlockSpec((1, CONV_W), const),
                  pl.BlockSpec((D_MODEL, D_MODEL), const), pl.BlockSpec((1, D_MODEL), const)],
        out_specs=pl.BlockSpec((tm, D_MODEL), row),
        out_shape=jax.ShapeDtypeStruct((n, D_MODEL), F32),
        compiler_params=pltpu.CompilerParams(
            dimension_semantics=("parallel",), vmem_limit_bytes=VMEM_LIMIT),
        name="mix_out",
    )(x2, att, conv, yc, ga, gb, w_bf16, gpost)


def _ffn_kernel(x_ref, gpre_ref, w1_ref, w2_ref, gpost_ref, o_ref, *, ff_tile):
    x = x_ref[...]
    h = _rms(x, gpre_ref[...]).astype(BF16)
    acc = None
    for j in range(D_FF // ff_tile):
        sl = slice(j * ff_tile, (j + 1) * ff_tile)
        a = jnp.maximum(_dot(h, w1_ref[:, sl]), 0.0)
        part = _dot((a * a).astype(BF16), w2_ref[sl, :])
        acc = part if acc is None else acc + part
    o_ref[...] = x + _rms(acc, gpost_ref[...])


def _ffn(x2, gpre, w1_bf16, w2_bf16, gpost, tm):
    n = x2.shape[0]
    row = lambda i: (i, 0)
    const = lambda i: (0, 0)
    return pl.pallas_call(
        functools.partial(_ffn_kernel, ff_tile=1024),
        grid=(n // tm,),
        in_specs=[pl.BlockSpec((tm, D_MODEL), row), pl.BlockSpec((1, D_MODEL), const),
                  pl.BlockSpec((D_MODEL, D_FF), const), pl.BlockSpec((D_FF, D_MODEL), const),
                  pl.BlockSpec((1, D_MODEL), const)],
        out_specs=pl.BlockSpec((tm, D_MODEL), row),
        out_shape=jax.ShapeDtypeStruct((n, D_MODEL), F32),
        compiler_params=pltpu.CompilerParams(
            dimension_semantics=("parallel",), vmem_limit_bytes=VMEM_LIMIT),
        name="ffn",
    )(x2, gpre, w1_bf16, w2_bf16, gpost)


def _pad_rows(w, start):
    return jnp.zeros((LORA_W, HEADS_W), F32).at[start:start + w.shape[0]].set(w)


def _layer_params(l, p):
    row = lambda a: a.reshape(1, -1).astype(F32)
    seg = np.kron(np.eye(N_HEADS, dtype=np.float32), np.ones((HEAD_DIM, HEAD_DIM), np.float32))
    return dict(
        g_mix_pre=row(p["norm_mix_pre"][l]), g_mix_post=row(p["norm_mix_post"][l]),
        g_ffn_pre=row(p["norm_ffn_pre"][l]), g_ffn_post=row(p["norm_ffn_post"][l]),
        w_in=p["w_in"][l].astype(BF16), w_out=p["w_out"][l].astype(BF16),
        attn_g=row(p["attn_out_g"][l]), conv_w=p["conv_w"][l].astype(F32),
        conv_g=row(p["conv_out_g"][l]), mu=row(p["rwkv_mu"][l]),
        w0=p["decay_w0"][l].astype(F32), a0=p["iclr_a0"][l].astype(F32),
        wdec=jnp.stack([_pad_rows(p["decay_up"][l, d], 0) for d in range(2)]).astype(BF16),
        wa=jnp.stack([_pad_rows(p["iclr_up"][l, d], DECAY_RANK) for d in range(2)]).astype(BF16),
        wg=_pad_rows(p["gate_up"][l], DECAY_RANK + ICLR_RANK).astype(BF16),
        k_k=row(p["k_k"][l]), k_a=row(p["k_a"][l]), r_k=row(p["r_k"][l]),
        lnx_w=row(p["lnx_w"][l]), lnx_b=row(p["lnx_b"][l]),
        w1=p["ffn_w1"][l].astype(BF16), w2=p["ffn_w2"][l].astype(BF16),
        seg=jnp.asarray(seg, BF16),
    )


def _layer(x2, lp, biases, b, t):
    n = b * t
    tm = min(512, n)
    tt = min(256, t)
    zq, zb, zc = _inproj(x2, lp["g_mix_pre"], lp["w_in"], tm)
    att = _attention(zq, biases, b, t)
    zc3 = zc.reshape(b, t, RWKV_IN)
    zb3 = zb.reshape(b, t, CONVIN_W)
    ys = [_scan(zc3, lp["mu"], lp["w0"][d:d + 1], lp["wdec"][d], lp["a0"][d:d + 1], lp["wa"][d],
                lp["k_k"], lp["k_a"], lp["seg"], b, t, tt, rev=bool(d)) for d in range(2)]
    yc, conv = _post(zc3, zb3, ys[0], ys[1], lp["mu"], lp["a0"], lp["wa"], lp["wg"], lp["k_a"],
                     lp["r_k"], lp["lnx_w"], lp["lnx_b"], lp["conv_w"], lp["seg"], b, t, tt)
    x2 = _mixout(x2, att, conv.reshape(n, CONV_W), yc.reshape(n, HEADS_W), lp["attn_g"],
                 lp["conv_g"], lp["w_out"], lp["g_mix_post"], tm)
    return _ffn(x2, lp["g_ffn_pre"], lp["w1"], lp["w2"], lp["g_ffn_post"], tm)


def _trunk(x, rel_bias, layers):
    b, t, _ = x.shape
    biases = [_attn_bias(rel_bias, t, dil) for dil in DILATIONS]
    x2 = x.reshape(b * t, D_MODEL)
    for lp in layers:
        x2 = _layer(x2, lp, biases, b, t)
    return x2.reshape(b, t, D_MODEL)


def kernel(x_prompt, x_sample, rel_bias, norm_mix_pre, norm_mix_post, norm_ffn_pre, norm_ffn_post, w_in, w_out, attn_out_g, conv_w, conv_out_g, rwkv_mu, decay_w0, decay_up, iclr_a0, iclr_up, gate_up, k_k, k_a, r_k, lnx_w, lnx_b, ffn_w1, ffn_w2):
    p = dict(norm_mix_pre=norm_mix_pre, norm_mix_post=norm_mix_post, norm_ffn_pre=norm_ffn_pre,
             norm_ffn_post=norm_ffn_post, w_in=w_in, w_out=w_out, attn_out_g=attn_out_g,
             conv_w=conv_w, conv_out_g=conv_out_g, rwkv_mu=rwkv_mu, decay_w0=decay_w0,
             decay_up=decay_up, iclr_a0=iclr_a0, iclr_up=iclr_up, gate_up=gate_up, k_k=k_k,
             k_a=k_a, r_k=r_k, lnx_w=lnx_w, lnx_b=lnx_b, ffn_w1=ffn_w1, ffn_w2=ffn_w2)
    layers = [_layer_params(l, p) for l in range(w_in.shape[0])]
    return (_trunk(x_prompt, rel_bias, layers), _trunk(x_sample, rel_bias, layers))
```

```python
import functools
import math

import numpy as np
import jax
import jax.numpy as jnp
from jax import lax
from jax.experimental import pallas as pl
from jax.experimental.pallas import tpu as pltpu

F32 = jnp.float32
BF16 = jnp.bfloat16

D_MODEL = 1024
HEAD_DIM = 64
N_HEADS = 6
HEADS_W = N_HEADS * HEAD_DIM
CONV_W = 256
QKV_W = 3 * HEADS_W
CONVIN_W = 3 * CONV_W
LORA_W = 128
DECAY_RANK = 32
ICLR_RANK = 32
RWKV_IN = 3 * HEADS_W + LORA_W
IN_WIDTH = QKV_W + CONVIN_W + RWKV_IN
D_FF = 4 * D_MODEL
DILATIONS = (1, 4, 16)
KEYS_PER_SIDE = 64
ATTN_UNROLL = 4
N_BUCKETS = 32
BUCKET_MAX_DIST = 1024
RMS_EPS = 1e-6
LNX_EPS = 64e-5
CHUNK = 64
LANES = 128
N_PAIRS = HEADS_W // LANES
HALO_ROWS = 8
VMEM_LIMIT = 56 * 1024 * 1024


def _rms(x, g, eps=RMS_EPS):
    return x * lax.rsqrt(jnp.mean(x * x, axis=-1, keepdims=True) + eps) * g


def _sigmoid(x):
    return 1.0 / (1.0 + jnp.exp(-x))


def _dot(a, b):
    return jnp.dot(a, b, preferred_element_type=F32)


def _dot_tb(a, b):
    return lax.dot_general(a, b, (((1,), (1,)), ((), ())), preferred_element_type=F32)


def _dot_ta(a, b):
    return lax.dot_general(a, b, (((0,), (0,)), ((), ())), preferred_element_type=F32)


def _split2(x):
    hi = x.astype(BF16)
    lo = (x - hi.astype(F32)).astype(BF16)
    return hi, lo


def _split3(x):
    hi = x.astype(BF16)
    r1 = x - hi.astype(F32)
    mid = r1.astype(BF16)
    lo = (r1 - mid.astype(F32)).astype(BF16)
    return hi, mid, lo


def _segsum(x, seg_ones):
    hi, lo = _split2(x)
    return _dot(hi, seg_ones) + _dot(lo, seg_ones)


def _neighbours(main, prev_row, next_row):
    tt = main.shape[0]
    row = lax.broadcasted_iota(jnp.int32, main.shape, 0)
    prev = jnp.where(row == 0, prev_row, pltpu.roll(main, 1, 0))
    nxt = jnp.where(row == tt - 1, next_row, pltpu.roll(main, tt - 1, 0))
    return prev, nxt


def _inproj_kernel(x_ref, xp_ref, xn_ref, g_ref, w_ref, mu_ref, cw_ref,
                   zq_ref, conv_ref, zs_ref, *, tm, t):
    i = pl.program_id(0)
    x_ext = jnp.concatenate([x_ref[...], xp_ref[...], xn_ref[...]], axis=0)
    h_ext = _rms(x_ext, g_ref[...]).astype(BF16)
    zq_ref[...] = _dot(h_ext[:tm], w_ref[:, 0:QKV_W])
    z_ext = _dot(h_ext, w_ref[:, QKV_W:IN_WIDTH])
    prev_row = jnp.where((i * tm) % t == 0, 0.0,
                         z_ext[tm + HALO_ROWS - 1:tm + HALO_ROWS, :])
    next_row = jnp.where(((i + 1) * tm) % t == 0, 0.0,
                         z_ext[tm + HALO_ROWS:tm + HALO_ROWS + 1, :])
    z = z_ext[:tm]

    gated = lambda a: a[:, CONV_W:2 * CONV_W] * a[:, 2 * CONV_W:3 * CONV_W]
    u = gated(z)
    u_prev, u_next = _neighbours(u, gated(prev_row), gated(next_row))
    conv_ref[...] = z[:, 0:CONV_W] * (
        u_prev * cw_ref[0:1, :] + u * cw_ref[1:2, :] + u_next * cw_ref[2:3, :])

    zc = z[:, CONVIN_W:]
    c_prev, c_next = _neighbours(zc, prev_row[:, CONVIN_W:], next_row[:, CONVIN_W:])
    zs_ref[...] = zc + (0.5 * (c_prev + c_next) - zc) * mu_ref[...]


def _inproj(x2, g, w_bf16, mu, cw, tm, t):
    n = x2.shape[0]
    per = tm // HALO_ROWS
    last = n // HALO_ROWS - 1
    row = lambda i: (i, 0)
    const = lambda i: (0, 0)
    return pl.pallas_call(
        functools.partial(_inproj_kernel, tm=tm, t=t),
        grid=(n // tm,),
        in_specs=[pl.BlockSpec((tm, D_MODEL), row),
                  pl.BlockSpec((HALO_ROWS, D_MODEL), lambda i: (jnp.maximum(i * per - 1, 0), 0)),
                  pl.BlockSpec((HALO_ROWS, D_MODEL), lambda i: (jnp.minimum((i + 1) * per, last), 0)),
                  pl.BlockSpec((1, D_MODEL), const),
                  pl.BlockSpec((D_MODEL, IN_WIDTH), const),
                  pl.BlockSpec((1, RWKV_IN), const),
                  pl.BlockSpec((3, CONV_W), const)],
        out_specs=[pl.BlockSpec((tm, QKV_W), row),
                   pl.BlockSpec((tm, CONV_W), row),
                   pl.BlockSpec((tm, RWKV_IN), row)],
        out_shape=[jax.ShapeDtypeStruct((n, QKV_W), F32),
                   jax.ShapeDtypeStruct((n, CONV_W), F32),
                   jax.ShapeDtypeStruct((n, RWKV_IN), F32)],
        compiler_params=pltpu.CompilerParams(
            dimension_semantics=("parallel",), vmem_limit_bytes=VMEM_LIMIT),
        name="inproj",
    )(x2, x2, x2, g, w_bf16, mu, cw)


def _t5_bucket(rel):
    half = N_BUCKETS // 2
    max_exact = half // 2
    ret = np.where(rel > 0, half, 0)
    n = np.abs(rel)
    large = max_exact + (np.log(np.maximum(n, 1) / max_exact)
                         / np.log(BUCKET_MAX_DIST / max_exact) * (half - max_exact)).astype(np.int32)
    large = np.minimum(large, half - 1)
    return (ret + np.where(n < max_exact, n, large)).astype(np.int32)


def _attn_cfg(t, dil):
    cls_len = t // dil
    bq = min(128, cls_len)
    wk = min(bq + 2 * KEYS_PER_SIDE, cls_len)
    return cls_len, bq, wk, cls_len // bq


def _attn_bias(rel_bias, t, dil):
    _, bq, wk, _ = _attn_cfg(t, dil)
    out = []
    for shift in (0, KEYS_PER_SIDE, wk - bq):
        rel = np.arange(wk)[None, :] - np.arange(bq)[:, None] - shift
        valid = np.abs(rel) <= KEYS_PER_SIDE
        onehot = jnp.asarray(_t5_bucket(rel * dil))[:, :, None] == jnp.arange(N_BUCKETS)
        table = jnp.transpose(rel_bias.astype(F32))[:, None, None, :]
        bias = jnp.sum(jnp.where(onehot[None], table, 0.0), axis=-1)
        out.append(jnp.where(valid[None], bias, -jnp.inf))
    return jnp.stack(out)


def _attn_kernel(q_ref, k_ref, v_ref, b0_ref, b1_ref, b2_ref, o_ref, m_ref, s_ref, *, t):
    lane = lax.broadcasted_iota(jnp.int32, (1, LANES), 1)
    head0 = lane < HEAD_DIM
    bias_refs = (b0_ref, b1_ref, b2_ref)

    for di, dil in enumerate(DILATIONS):
        cls_len, bq, wk, nb = _attn_cfg(t, dil)
        bias_ref = bias_refs[di]

        def rows(start, size, dil=dil):
            if dil == 1:
                return pl.ds(start, size)
            return pl.ds(start, size, stride=dil)

        def do_blocks(blocks, di=di, dil=dil, bq=bq, wk=wk, cls_len=cls_len,
                      bias_ref=bias_ref, rows=rows):
            qrows, q, kw, vw = [], [], [], []
            for c, blk, placement in blocks:
                m0 = blk * bq
                ws = (0, m0 - KEYS_PER_SIDE, cls_len - wk)[placement]
                qrows.append(rows(c + dil * m0, bq))
                krows = rows(c + dil * ws, wk)
                q.append(q_ref[qrows[-1], :] * (HEAD_DIM ** -0.5))
                kw.append(k_ref[krows, :].astype(BF16))
                vw.append(v_ref[krows, :].astype(BF16))
            items = [(i, h) for i in range(len(blocks)) for h in range(2)]
            hmask = (head0, jnp.logical_not(head0))
            logits = {(i, h): _dot_tb(jnp.where(hmask[h], q[i], 0.0).astype(BF16), kw[i])
                      + bias_ref[blocks[i][2], h] for i, h in items}
            mh = {k: jnp.max(logits[k], axis=-1, keepdims=True) for k in items}
            p = {k: jnp.exp(logits[k] - mh[k]) for k in items}
            sh = {k: jnp.sum(p[k], axis=-1, keepdims=True) for k in items}
            oh = {(i, h): _dot(p[(i, h)].astype(BF16), vw[i]) for i, h in items}
            for i in range(len(blocks)):
                m_blk = jnp.where(head0, mh[(i, 0)], mh[(i, 1)])
                s_blk = jnp.where(head0, sh[(i, 0)], sh[(i, 1)])
                o_blk = jnp.where(head0, oh[(i, 0)], oh[(i, 1)])
                if di == 0:
                    m_ref[qrows[i], :] = m_blk
                    s_ref[qrows[i], :] = s_blk
                    o_ref[qrows[i], :] = o_blk
                else:
                    m_old = m_ref[qrows[i], :]
                    m_new = jnp.maximum(m_old, m_blk)
                    a_old = jnp.exp(m_old - m_new)
                    a_blk = jnp.exp(m_blk - m_new)
                    m_ref[qrows[i], :] = m_new
                    s_ref[qrows[i], :] = s_ref[qrows[i], :] * a_old + s_blk * a_blk
                    o_ref[qrows[i], :] = o_ref[qrows[i], :] * a_old + o_blk * a_blk

        def loop(n, body):
            if n == 1:
                body(0)
            else:
                lax.fori_loop(0, n, lambda i, carry: (body(i), carry)[1], 0)

        edge = [(0, 0)] + ([(nb - 1, 2)] if nb > 1 else [])
        cb = math.gcd(dil, max(1, ATTN_UNROLL // len(edge)))
        loop(dil // cb, lambda i, edge=edge, cb=cb, do_blocks=do_blocks: do_blocks(
            [(i * cb + j, blk, placement) for j in range(cb) for blk, placement in edge]))
        n_mid = max(nb - 2, 0)
        if n_mid:
            u = max(d for d in range(1, ATTN_UNROLL + 1) if n_mid % d == 0)
            per = n_mid // u
            loop(dil * per, lambda i, u=u, per=per, do_blocks=do_blocks: do_blocks(
                [(i // per, 1 + (i % per) * u + j, 1) for j in range(u)]))

    rb = min(256, t)

    def norm_body(i, carry):
        sl = pl.ds(pl.multiple_of(i * rb, rb), rb)
        o_ref[sl, :] = o_ref[sl, :] / s_ref[sl, :]
        return carry

    lax.fori_loop(0, t // rb, norm_body, 0)


def _attention(zq, biases, b, t):
    zq3 = zq.reshape(b, t, QKV_W)
    in_specs = [pl.BlockSpec((None, t, LANES), lambda i, hp: (i, 0, hp)),
                pl.BlockSpec((None, t, LANES), lambda i, hp: (i, 0, N_PAIRS + hp)),
                pl.BlockSpec((None, t, LANES), lambda i, hp: (i, 0, 2 * N_PAIRS + hp))]
    for bias in biases:
        in_specs.append(pl.BlockSpec((3, 2) + bias.shape[2:], lambda i, hp: (0, hp, 0, 0)))
    out = pl.pallas_call(
        functools.partial(_attn_kernel, t=t),
        grid=(b, N_PAIRS),
        in_specs=in_specs,
        out_specs=pl.BlockSpec((None, t, LANES), lambda i, hp: (i, 0, hp)),
        out_shape=jax.ShapeDtypeStruct((b, t, HEADS_W), F32),
        scratch_shapes=[pltpu.VMEM((t, LANES), F32), pltpu.VMEM((t, LANES), F32)],
        compiler_params=pltpu.CompilerParams(
            dimension_semantics=("parallel", "parallel"), vmem_limit_bytes=VMEM_LIMIT),
        name="dilated_attention",
    )(zq3, zq3, zq3, *biases)
    return out.reshape(b * t, HEADS_W)


def _scan_kernel(zs_ref, w0_ref, wdec_ref, a0_ref, wa_ref,
                 kk_ref, ka_ref, seg_ref, y_ref, state_ref, *, tt, rev):
    @pl.when(pl.program_id(1) == 0)
    def _():
        state_ref[...] = jnp.zeros_like(state_ref)

    r = zs_ref[:, 0:HEADS_W]
    k = zs_ref[:, HEADS_W:2 * HEADS_W]
    v = zs_ref[:, 2 * HEADS_W:3 * HEADS_W]
    lora = zs_ref[:, 3 * HEADS_W:RWKV_IN]

    xdec = w0_ref[...] + _dot(jnp.tanh(lora).astype(BF16), wdec_ref[...])
    lw = (-math.exp(-0.5)) * _sigmoid(xdec)
    asig = _sigmoid(a0_ref[...] + _dot(lora.astype(BF16), wa_ref[...]))
    kk = k * kk_ref[...]
    kk = kk / jnp.maximum(jnp.sqrt(_segsum(kk * kk, seg_ref[...])), 1e-12)
    kdir = k * (1.0 + (asig - 1.0) * ka_ref[...])
    bvec = kk * asig

    n_chunks = tt // CHUNK
    pack = lambda x: jnp.concatenate(
        [x[c * CHUNK:(c + 1) * CHUNK] for c in range(n_chunks)], axis=1)
    lw_p, r_p, kd_p, bv_p, kk_p, v_p = (pack(x) for x in (lw, r, kdir, bvec, kk, v))

    trow = lax.broadcasted_iota(jnp.int32, (CHUNK, CHUNK), 0)
    tcol = lax.broadcasted_iota(jnp.int32, (CHUNK, CHUNK), 1)
    tri = jnp.where((tcol >= trow) if rev else (tcol <= trow), 1.0, 0.0).astype(BF16)
    lw3 = _split3(lw_p)
    cum = _dot(tri, lw3[0]) + _dot(tri, lw3[1]) + _dot(tri, lw3[2])
    last = 0 if rev else CHUNK - 1
    tot = cum[last:last + 1, :]
    e_neg = jnp.exp(-cum)
    e_end = jnp.exp(tot - cum)
    r_t = r_p * jnp.exp(cum)
    k_t = kd_p * e_neg
    b_t = bv_p * e_neg
    a_t = -kk_p * jnp.exp(cum - lw_p)
    b_e = bv_p * e_end
    k_e = kd_p * e_end
    p_end = jnp.exp(tot)

    gw = 2 * LANES
    head0 = lax.broadcasted_iota(jnp.int32, (1, LANES), 1) < HEAD_DIM
    tpos = lax.broadcasted_iota(jnp.int32, (CHUNK, gw), 0)
    spos = lax.broadcasted_iota(jnp.int32, (CHUNK, gw), 1) % CHUNK
    if rev:
        incl, strict = spos >= tpos, spos > tpos
    else:
        incl, strict = spos <= tpos, spos < tpos
    eye_g = jnp.where(spos == tpos, 1.0, 0.0)
    prow = lax.broadcasted_iota(jnp.int32, (LANES, LANES), 0)
    pcol = lax.broadcasted_iota(jnp.int32, (LANES, LANES), 1)
    pair_diag = (prow // HEAD_DIM) == (pcol // HEAD_DIM)

    def bd(xb):
        lo, hi = xb[:, :LANES], xb[:, LANES:]
        z = jnp.zeros_like(lo)
        return jnp.concatenate([
            jnp.concatenate([jnp.where(head0, lo, z), z], axis=1),
            jnp.concatenate([jnp.where(head0, z, lo), z], axis=1),
            jnp.concatenate([z, jnp.where(head0, hi, z)], axis=1),
            jnp.concatenate([z, jnp.where(head0, z, hi)], axis=1)], axis=0)

    order = range(n_chunks - 1, -1, -1) if rev else range(n_chunks)

    def piece(x, p, c):
        lane0 = c * HEADS_W + p * LANES
        return x[:, lane0:lane0 + LANES]

    def grp(x, key):
        p, g = key
        return jnp.concatenate([piece(x, p, 2 * g), piece(x, p, 2 * g + 1)], axis=1)

    groups = [(p, g) for p in range(N_PAIRS) for g in range(n_chunks // 2)]
    pairs = range(N_PAIRS)
    rg = {k: grp(r_t, k) for k in groups}
    ag_b = {k: grp(a_t, k).astype(BF16) for k in groups}
    lhs = {k: jnp.concatenate([ag_b[k], rg[k].astype(BF16)], axis=0) for k in groups}
    s1 = {k: _dot_tb(lhs[k], bd(grp(b_t, k).astype(BF16))) for k in groups}
    s2 = {k: _dot_tb(lhs[k], bd(grp(k_t, k).astype(BF16))) for k in groups}
    a_ab = {k: jnp.where(strict, s1[k][:CHUNK], 0.0) for k in groups}
    a_rb = {k: jnp.where(incl, s1[k][CHUNK:], 0.0).astype(BF16) for k in groups}
    a_ak = {k: jnp.where(strict, s2[k][:CHUNK], 0.0).astype(BF16) for k in groups}
    a_rk = {k: jnp.where(incl, s2[k][CHUNK:], 0.0).astype(BF16) for k in groups}
    def sibling(m):
        return jnp.logical_and(tpos // (2 * m) == spos // (2 * m), tpos // m != spos // m)

    tinv = {k: eye_g + jnp.where(sibling(1), a_ab[k], 0.0) for k in groups}
    m = 2
    while m < CHUNK:
        a_off = {k: jnp.where(sibling(m), a_ab[k], 0.0).astype(BF16) for k in groups}
        db = {k: tinv[k].astype(BF16) for k in groups}
        yb = {k: _dot(a_off[k], bd(db[k])).astype(BF16) for k in groups}
        tinv = {k: tinv[k] + _dot(db[k], bd(yb[k])) for k in groups}
        m *= 2
    tb = {k: tinv[k].astype(BF16) for k in groups}
    vbd = {k: bd(grp(v_p, k).astype(BF16)) for k in groups}
    w2 = {k: _dot(a_ak[k], vbd[k]).astype(BF16) for k in groups}
    ta = {k: _dot(tb[k], bd(ag_b[k])) for k in groups}
    tv = {k: _dot(tb[k], bd(w2[k])) for k in groups}
    qg = {k: rg[k] + _dot(a_rb[k], bd(ta[k].astype(BF16))) for k in groups}
    yg = {k: _dot(a_rb[k], bd(tv[k].astype(BF16))) + _dot(a_rk[k], vbd[k]) for k in groups}

    def chunk_of(d, p, c):
        half = slice((c % 2) * LANES, (c % 2 + 1) * LANES)
        return d[(p, c // 2)][:, half]

    pcs = [(p, c) for c in order for p in pairs]
    be = {pc: piece(b_e, *pc) for pc in pcs}
    xt = {pc: jnp.where(pair_diag, _dot_ta(chunk_of(ta, *pc).astype(BF16),
                                           be[pc].astype(BF16)), 0.0).astype(BF16)
          for pc in pcs}
    ht = {pc: jnp.where(pair_diag, _dot_ta(
        jnp.concatenate([chunk_of(tv, *pc), piece(v_p, *pc)], axis=0).astype(BF16),
        jnp.concatenate([be[pc], piece(k_e, *pc)], axis=0).astype(BF16)), 0.0) for pc in pcs}

    state = {p: state_ref[p] for p in pairs}
    for ci in order:
        sb = {p: state[p].astype(BF16) for p in pairs}
        for p in pairs:
            y_ref[ci * CHUNK:(ci + 1) * CHUNK, p * LANES:(p + 1) * LANES] = (
                chunk_of(yg, p, ci) + _dot_tb(chunk_of(qg, p, ci).astype(BF16), sb[p]))
        state = {p: state[p] * piece(p_end, p, ci) + _dot(sb[p], xt[(p, ci)]) + ht[(p, ci)]
                 for p in pairs}
    for p in pairs:
        state_ref[p] = state[p]


def _scan(zs3, w0, wdec, a0, wa, k_k, k_a, seg, b, t, tt, rev):
    nt = t // tt
    tile = (lambda i, s: (i, nt - 1 - s, 0)) if rev else (lambda i, s: (i, s, 0))
    const = lambda i, s: (0, 0)
    vec = lambda w: pl.BlockSpec((1, w), const)
    in_specs = [pl.BlockSpec((None, tt, RWKV_IN), tile),
                vec(HEADS_W), pl.BlockSpec((LORA_W, HEADS_W), const),
                vec(HEADS_W), pl.BlockSpec((LORA_W, HEADS_W), const),
                vec(HEADS_W), vec(HEADS_W), pl.BlockSpec((HEADS_W, HEADS_W), const)]
    return pl.pallas_call(
        functools.partial(_scan_kernel, tt=tt, rev=rev),
        grid=(b, nt),
        in_specs=in_specs,
        out_specs=pl.BlockSpec((None, tt, HEADS_W), tile),
        out_shape=jax.ShapeDtypeStruct((b, t, HEADS_W), F32),
        scratch_shapes=[pltpu.VMEM((N_PAIRS, LANES, LANES), F32)],
        compiler_params=pltpu.CompilerParams(
            dimension_semantics=("parallel", "arbitrary"), vmem_limit_bytes=VMEM_LIMIT),
        name="rwkv_scan_bwd" if rev else "rwkv_scan_fwd",
    )(zs3, w0, wdec, a0, wa, k_k, k_a, seg)


def _mix_ffn_kernel(x_ref, att_ref, conv_ref, zs_ref, yf_ref, yb_ref,
                    a0_ref, wa_ref, wg_ref, ka_ref, rk_ref, lnw_ref, lnb_ref, seg_ref,
                    ga_ref, gb_ref, wout_ref, gmix_ref, gpre_ref, w1_ref, w2_ref, gffn_ref,
                    o_ref, *, ff_tile):
    seg = seg_ref[...]

    r = zs_ref[:, 0:HEADS_W]
    k = zs_ref[:, HEADS_W:2 * HEADS_W]
    v = zs_ref[:, 2 * HEADS_W:3 * HEADS_W]
    lora = zs_ref[:, 3 * HEADS_W:RWKV_IN]
    lora_b = lora.astype(BF16)
    asum = (_sigmoid(a0_ref[0:1, :] + _dot(lora_b, wa_ref[0]))
            + _sigmoid(a0_ref[1:2, :] + _dot(lora_b, wa_ref[1])))
    gate = _dot(_sigmoid(lora).astype(BF16), wg_ref[...])
    ksum = k * (2.0 + (asum - 2.0) * ka_ref[...])
    bonus = _segsum(r * ksum * rk_ref[...], seg) * v
    y = yf_ref[...] + yb_ref[...]
    yc = y - _segsum(y, seg) * (1.0 / HEAD_DIM)
    var = _segsum(yc * yc, seg) * (1.0 / HEAD_DIM)
    yn = yc * lax.rsqrt(var + LNX_EPS) * lnw_ref[...] + lnb_ref[...]
    y_rwkv = ((yn + bonus) * gate).astype(BF16)

    ya = _rms(att_ref[...], ga_ref[...]).astype(BF16)
    yb = _rms(conv_ref[...], gb_ref[...]).astype(BF16)
    mix = (_dot(ya, wout_ref[0:HEADS_W, :])
           + _dot(yb, wout_ref[HEADS_W:HEADS_W + CONV_W, :])
           + _dot(y_rwkv, wout_ref[HEADS_W + CONV_W:, :]))
    x = x_ref[...] + _rms(mix, gmix_ref[...])

    h = _rms(x, gpre_ref[...]).astype(BF16)
    acc = None
    for j in range(D_FF // ff_tile):
        sl = slice(j * ff_tile, (j + 1) * ff_tile)
        a = jnp.maximum(_dot(h, w1_ref[:, sl]), 0.0)
        part = _dot((a * a).astype(BF16), w2_ref[sl, :])
        acc = part if acc is None else acc + part
    o_ref[...] = x + _rms(acc, gffn_ref[...])


def _mix_ffn(x2, att, conv, zs, yf, yb, lp, tm):
    n = x2.shape[0]
    row = lambda i: (i, 0)
    tile = lambda w: pl.BlockSpec((tm, w), row)
    fixed = lambda *shape: pl.BlockSpec(shape, lambda i: (0,) * len(shape),
                                        pipeline_mode=pl.Buffered(1))
    return pl.pallas_call(
        functools.partial(_mix_ffn_kernel, ff_tile=1024),
        grid=(n // tm,),
        in_specs=[tile(D_MODEL), tile(HEADS_W), tile(CONV_W), tile(RWKV_IN), tile(HEADS_W),
                  tile(HEADS_W),
                  fixed(2, HEADS_W), fixed(2, LORA_W, HEADS_W), fixed(LORA_W, HEADS_W),
                  fixed(1, HEADS_W), fixed(1, HEADS_W), fixed(1, HEADS_W), fixed(1, HEADS_W),
                  fixed(HEADS_W, HEADS_W),
                  fixed(1, HEADS_W), fixed(1, CONV_W), fixed(D_MODEL, D_MODEL),
                  fixed(1, D_MODEL), fixed(1, D_MODEL), fixed(D_MODEL, D_FF),
                  fixed(D_FF, D_MODEL), fixed(1, D_MODEL)],
        out_specs=tile(D_MODEL),
        out_shape=jax.ShapeDtypeStruct((n, D_MODEL), F32),
        compiler_params=pltpu.CompilerParams(
            dimension_semantics=("parallel",), vmem_limit_bytes=VMEM_LIMIT),
        name="mix_ffn",
    )(x2, att, conv, zs, yf, yb,
      lp["a0"], lp["wa"], lp["wg"], lp["k_a"], lp["r_k"], lp["lnx_w"], lp["lnx_b"], lp["seg"],
      lp["attn_g"], lp["conv_g"], lp["w_out"], lp["g_mix_post"], lp["g_ffn_pre"],
      lp["w1"], lp["w2"], lp["g_ffn_post"])


def _pad_rows(w, start):
    return jnp.zeros((LORA_W, HEADS_W), F32).at[start:start + w.shape[0]].set(w)


def _layer_params(l, p):
    row = lambda a: a.reshape(1, -1).astype(F32)
    seg = np.kron(np.eye(N_HEADS, dtype=np.float32), np.ones((HEAD_DIM, HEAD_DIM), np.float32))
    return dict(
        g_mix_pre=row(p["norm_mix_pre"][l]), g_mix_post=row(p["norm_mix_post"][l]),
        g_ffn_pre=row(p["norm_ffn_pre"][l]), g_ffn_post=row(p["norm_ffn_post"][l]),
        w_in=p["w_in"][l].astype(BF16), w_out=p["w_out"][l].astype(BF16),
        attn_g=row(p["attn_out_g"][l]), conv_w=p["conv_w"][l].astype(F32),
        conv_g=row(p["conv_out_g"][l]), mu=row(p["rwkv_mu"][l]),
        w0=p["decay_w0"][l].astype(F32), a0=p["iclr_a0"][l].astype(F32),
        wdec=jnp.stack([_pad_rows(p["decay_up"][l, d], 0) for d in range(2)]).astype(BF16),
        wa=jnp.stack([_pad_rows(p["iclr_up"][l, d], DECAY_RANK) for d in range(2)]).astype(BF16),
        wg=_pad_rows(p["gate_up"][l], DECAY_RANK + ICLR_RANK).astype(BF16),
        k_k=row(p["k_k"][l]), k_a=row(p["k_a"][l]), r_k=row(p["r_k"][l]),
        lnx_w=row(p["lnx_w"][l]), lnx_b=row(p["lnx_b"][l]),
        w1=p["ffn_w1"][l].astype(BF16), w2=p["ffn_w2"][l].astype(BF16),
        seg=jnp.asarray(seg, BF16),
    )


def _layer(x2, lp, biases, b, t):
    n = b * t
    tm = min(512, n)
    tt = min(256, t)
    zq, conv, zs = _inproj(x2, lp["g_mix_pre"], lp["w_in"], lp["mu"], lp["conv_w"], tm, t)
    att = _attention(zq, biases, b, t)
    zs3 = zs.reshape(b, t, RWKV_IN)
    ys = [_scan(zs3, lp["w0"][d:d + 1], lp["wdec"][d], lp["a0"][d:d + 1], lp["wa"][d],
                lp["k_k"], lp["k_a"], lp["seg"], b, t, tt, rev=bool(d)).reshape(n, HEADS_W)
          for d in range(2)]
    return _mix_ffn(x2, att, conv, zs, ys[0], ys[1], lp, tm)


def _trunk(x, rel_bias, layers):
    b, t, _ = x.shape
    biases = [_attn_bias(rel_bias, t, dil) for dil in DILATIONS]
    x2 = x.reshape(b * t, D_MODEL)
    for lp in layers:
        x2 = _layer(x2, lp, biases, b, t)
    return x2.reshape(b, t, D_MODEL)


def kernel(x_prompt, x_sample, rel_bias, norm_mix_pre, norm_mix_post, norm_ffn_pre, norm_ffn_post, w_in, w_out, attn_out_g, conv_w, conv_out_g, rwkv_mu, decay_w0, decay_up, iclr_a0, iclr_up, gate_up, k_k, k_a, r_k, lnx_w, lnx_b, ffn_w1, ffn_w2):
    p = dict(norm_mix_pre=norm_mix_pre, norm_mix_post=norm_mix_post, norm_ffn_pre=norm_ffn_pre,
             norm_ffn_post=norm_ffn_post, w_in=w_in, w_out=w_out, attn_out_g=attn_out_g,
             conv_w=conv_w, conv_out_g=conv_out_g, rwkv_mu=rwkv_mu, decay_w0=decay_w0,
             decay_up=decay_up, iclr_a0=iclr_a0, iclr_up=iclr_up, gate_up=gate_up, k_k=k_k,
             k_a=k_a, r_k=r_k, lnx_w=lnx_w, lnx_b=lnx_b, ffn_w1=ffn_w1, ffn_w2=ffn_w2)
    layers = [_layer_params(l, p) for l in range(w_in.shape[0])]
    return (_trunk(x_prompt, rel_bias, layers), _trunk(x_sample, rel_bias, layers))
```

```python
import functools
import math

import numpy as np
import jax
import jax.numpy as jnp
from jax import lax
from jax.experimental import pallas as pl
from jax.experimental.pallas import tpu as pltpu

F32 = jnp.float32
BF16 = jnp.bfloat16

D_MODEL = 1024
HEAD_DIM = 64
N_HEADS = 6
HEADS_W = N_HEADS * HEAD_DIM
CONV_W = 256
QKV_W = 3 * HEADS_W
CONVIN_W = 3 * CONV_W
LORA_W = 128
DECAY_RANK = 32
ICLR_RANK = 32
RWKV_IN = 3 * HEADS_W + LORA_W
IN_WIDTH = QKV_W + CONVIN_W + RWKV_IN
D_FF = 4 * D_MODEL
DILATIONS = (1, 4, 16)
KEYS_PER_SIDE = 64
ATTN_UNROLL = 4
SCAN_TILE = 512
SCAN_CHAIN_EVERY = 2
_DONE = object()
N_BUCKETS = 32
BUCKET_MAX_DIST = 1024
RMS_EPS = 1e-6
LNX_EPS = 64e-5
CHUNK = 64
LANES = 128
N_PAIRS = HEADS_W // LANES
HALO_ROWS = 8
VMEM_LIMIT = 56 * 1024 * 1024


def _rms(x, g, eps=RMS_EPS):
    return x * lax.rsqrt(jnp.mean(x * x, axis=-1, keepdims=True) + eps) * g


def _sigmoid(x):
    return 1.0 / (1.0 + jnp.exp(-x))


def _dot(a, b):
    return jnp.dot(a, b, preferred_element_type=F32)


def _dot_tb(a, b):
    return lax.dot_general(a, b, (((1,), (1,)), ((), ())), preferred_element_type=F32)


def _dot_ta(a, b):
    return lax.dot_general(a, b, (((0,), (0,)), ((), ())), preferred_element_type=F32)


def _split2(x):
    hi = x.astype(BF16)
    lo = (x - hi.astype(F32)).astype(BF16)
    return hi, lo


def _split3(x):
    hi = x.astype(BF16)
    r1 = x - hi.astype(F32)
    mid = r1.astype(BF16)
    lo = (r1 - mid.astype(F32)).astype(BF16)
    return hi, mid, lo


def _segsum(x, seg_ones):
    hi, lo = _split2(x)
    return _dot(hi, seg_ones) + _dot(lo, seg_ones)


def _neighbours(main, prev_row, next_row):
    tt = main.shape[0]
    row = lax.broadcasted_iota(jnp.int32, main.shape, 0)
    prev = jnp.where(row == 0, prev_row, pltpu.roll(main, 1, 0))
    nxt = jnp.where(row == tt - 1, next_row, pltpu.roll(main, tt - 1, 0))
    return prev, nxt


def _inproj_kernel(x_ref, xp_ref, xn_ref, g_ref, w_ref, mu_ref, cw_ref,
                   zq_ref, conv_ref, zs_ref, *, tm, t):
    i = pl.program_id(0)
    x_ext = jnp.concatenate([x_ref[...], xp_ref[...], xn_ref[...]], axis=0)
    h_ext = _rms(x_ext, g_ref[...]).astype(BF16)
    zq_ref[...] = _dot(h_ext[:tm], w_ref[:, 0:QKV_W])
    z_ext = _dot(h_ext, w_ref[:, QKV_W:IN_WIDTH])
    prev_row = jnp.where((i * tm) % t == 0, 0.0,
                         z_ext[tm + HALO_ROWS - 1:tm + HALO_ROWS, :])
    next_row = jnp.where(((i + 1) * tm) % t == 0, 0.0,
                         z_ext[tm + HALO_ROWS:tm + HALO_ROWS + 1, :])
    z = z_ext[:tm]

    gated = lambda a: a[:, CONV_W:2 * CONV_W] * a[:, 2 * CONV_W:3 * CONV_W]
    u = gated(z)
    u_prev, u_next = _neighbours(u, gated(prev_row), gated(next_row))
    conv_ref[...] = z[:, 0:CONV_W] * (
        u_prev * cw_ref[0:1, :] + u * cw_ref[1:2, :] + u_next * cw_ref[2:3, :])

    zc = z[:, CONVIN_W:]
    c_prev, c_next = _neighbours(zc, prev_row[:, CONVIN_W:], next_row[:, CONVIN_W:])
    zs_ref[...] = zc + (0.5 * (c_prev + c_next) - zc) * mu_ref[...]


def _inproj(x2, g, w_bf16, mu, cw, tm, t):
    n = x2.shape[0]
    per = tm // HALO_ROWS
    last = n // HALO_ROWS - 1
    row = lambda i: (i, 0)
    const = lambda i: (0, 0)
    return pl.pallas_call(
        functools.partial(_inproj_kernel, tm=tm, t=t),
        grid=(n // tm,),
        in_specs=[pl.BlockSpec((tm, D_MODEL), row),
                  pl.BlockSpec((HALO_ROWS, D_MODEL), lambda i: (jnp.maximum(i * per - 1, 0), 0)),
                  pl.BlockSpec((HALO_ROWS, D_MODEL), lambda i: (jnp.minimum((i + 1) * per, last), 0)),
                  pl.BlockSpec((1, D_MODEL), const),
                  pl.BlockSpec((D_MODEL, IN_WIDTH), const),
                  pl.BlockSpec((1, RWKV_IN), const),
                  pl.BlockSpec((3, CONV_W), const)],
        out_specs=[pl.BlockSpec((tm, QKV_W), row),
                   pl.BlockSpec((tm, CONV_W), row),
                   pl.BlockSpec((tm, RWKV_IN), row)],
        out_shape=[jax.ShapeDtypeStruct((n, QKV_W), F32),
                   jax.ShapeDtypeStruct((n, CONV_W), F32),
                   jax.ShapeDtypeStruct((n, RWKV_IN), F32)],
        compiler_params=pltpu.CompilerParams(
            dimension_semantics=("parallel",), vmem_limit_bytes=VMEM_LIMIT),
        name="inproj",
    )(x2, x2, x2, g, w_bf16, mu, cw)


def _t5_bucket(rel):
    half = N_BUCKETS // 2
    max_exact = half // 2
    ret = np.where(rel > 0, half, 0)
    n = np.abs(rel)
    large = max_exact + (np.log(np.maximum(n, 1) / max_exact)
                         / np.log(BUCKET_MAX_DIST / max_exact) * (half - max_exact)).astype(np.int32)
    large = np.minimum(large, half - 1)
    return (ret + np.where(n < max_exact, n, large)).astype(np.int32)


def _attn_cfg(t, dil):
    cls_len = t // dil
    bq = min(128, cls_len)
    wk = min(bq + 2 * KEYS_PER_SIDE, cls_len)
    return cls_len, bq, wk, cls_len // bq


def _attn_bias(rel_bias, t, dil):
    _, bq, wk, _ = _attn_cfg(t, dil)
    out = []
    for shift in (0, KEYS_PER_SIDE, wk - bq):
        rel = np.arange(wk)[None, :] - np.arange(bq)[:, None] - shift
        valid = np.abs(rel) <= KEYS_PER_SIDE
        onehot = jnp.asarray(_t5_bucket(rel * dil))[:, :, None] == jnp.arange(N_BUCKETS)
        table = jnp.transpose(rel_bias.astype(F32))[:, None, None, :]
        bias = jnp.sum(jnp.where(onehot[None], table, 0.0), axis=-1)
        out.append(jnp.where(valid[None], bias, -jnp.inf))
    return jnp.stack(out)


def _attn_kernel(q_ref, k_ref, v_ref, b0_ref, b1_ref, b2_ref, o_ref, m_ref, s_ref, *, t):
    lane = lax.broadcasted_iota(jnp.int32, (1, LANES), 1)
    head0 = lane < HEAD_DIM
    bias_refs = (b0_ref, b1_ref, b2_ref)

    for di, dil in enumerate(DILATIONS):
        cls_len, bq, wk, nb = _attn_cfg(t, dil)
        bias_ref = bias_refs[di]

        def rows(start, size, dil=dil):
            if dil == 1:
                return pl.ds(start, size)
            return pl.ds(start, size, stride=dil)

        def do_blocks(blocks, di=di, dil=dil, bq=bq, wk=wk, cls_len=cls_len,
                      bias_ref=bias_ref, rows=rows):
            qrows, q, kw, vw = [], [], [], []
            for c, blk, placement in blocks:
                m0 = blk * bq
                ws = (0, m0 - KEYS_PER_SIDE, cls_len - wk)[placement]
                qrows.append(rows(c + dil * m0, bq))
                krows = rows(c + dil * ws, wk)
                q.append(q_ref[qrows[-1], :] * (HEAD_DIM ** -0.5))
                kw.append(k_ref[krows, :].astype(BF16))
                vw.append(v_ref[krows, :].astype(BF16))
            items = [(i, h) for i in range(len(blocks)) for h in range(2)]
            hmask = (head0, jnp.logical_not(head0))
            logits = {(i, h): _dot_tb(jnp.where(hmask[h], q[i], 0.0).astype(BF16), kw[i])
                      + bias_ref[blocks[i][2], h] for i, h in items}
            mh = {k: jnp.max(logits[k], axis=-1, keepdims=True) for k in items}
            p = {k: jnp.exp(logits[k] - mh[k]) for k in items}
            sh = {k: jnp.sum(p[k], axis=-1, keepdims=True) for k in items}
            oh = {(i, h): _dot(p[(i, h)].astype(BF16), vw[i]) for i, h in items}
            for i in range(len(blocks)):
                m_blk = jnp.where(head0, mh[(i, 0)], mh[(i, 1)])
                s_blk = jnp.where(head0, sh[(i, 0)], sh[(i, 1)])
                o_blk = jnp.where(head0, oh[(i, 0)], oh[(i, 1)])
                if di == 0:
                    m_ref[qrows[i], :] = m_blk
                    s_ref[qrows[i], :] = s_blk
                    o_ref[qrows[i], :] = o_blk
                else:
                    m_old = m_ref[qrows[i], :]
                    m_new = jnp.maximum(m_old, m_blk)
                    a_old = jnp.exp(m_old - m_new)
                    a_blk = jnp.exp(m_blk - m_new)
                    m_ref[qrows[i], :] = m_new
                    s_ref[qrows[i], :] = s_ref[qrows[i], :] * a_old + s_blk * a_blk
                    o_ref[qrows[i], :] = o_ref[qrows[i], :] * a_old + o_blk * a_blk

        def loop(n, body):
            if n == 1:
                body(0)
            else:
                lax.fori_loop(0, n, lambda i, carry: (body(i), carry)[1], 0)

        edge = [(0, 0)] + ([(nb - 1, 2)] if nb > 1 else [])
        cb = math.gcd(dil, max(1, ATTN_UNROLL // len(edge)))
        loop(dil // cb, lambda i, edge=edge, cb=cb, do_blocks=do_blocks: do_blocks(
            [(i * cb + j, blk, placement) for j in range(cb) for blk, placement in edge]))
        n_mid = max(nb - 2, 0)
        if n_mid:
            u = max(d for d in range(1, ATTN_UNROLL + 1) if n_mid % d == 0)
            per = n_mid // u
            loop(dil * per, lambda i, u=u, per=per, do_blocks=do_blocks: do_blocks(
                [(i // per, 1 + (i % per) * u + j, 1) for j in range(u)]))

    rb = min(256, t)

    def norm_body(i, carry):
        sl = pl.ds(pl.multiple_of(i * rb, rb), rb)
        o_ref[sl, :] = o_ref[sl, :] / s_ref[sl, :]
        return carry

    lax.fori_loop(0, t // rb, norm_body, 0)


def _attention(zq, biases, b, t):
    zq3 = zq.reshape(b, t, QKV_W)
    in_specs = [pl.BlockSpec((None, t, LANES), lambda i, hp: (i, 0, hp)),
                pl.BlockSpec((None, t, LANES), lambda i, hp: (i, 0, N_PAIRS + hp)),
                pl.BlockSpec((None, t, LANES), lambda i, hp: (i, 0, 2 * N_PAIRS + hp))]
    for bias in biases:
        in_specs.append(pl.BlockSpec((3, 2) + bias.shape[2:], lambda i, hp: (0, hp, 0, 0)))
    out = pl.pallas_call(
        functools.partial(_attn_kernel, t=t),
        grid=(b, N_PAIRS),
        in_specs=in_specs,
        out_specs=pl.BlockSpec((None, t, LANES), lambda i, hp: (i, 0, hp)),
        out_shape=jax.ShapeDtypeStruct((b, t, HEADS_W), F32),
        scratch_shapes=[pltpu.VMEM((t, LANES), F32), pltpu.VMEM((t, LANES), F32)],
        compiler_params=pltpu.CompilerParams(
            dimension_semantics=("parallel", "parallel"), vmem_limit_bytes=VMEM_LIMIT),
        name="dilated_attention",
    )(zq3, zq3, zq3, *biases)
    return out.reshape(b * t, HEADS_W)


_PRE_BF16 = ("r_t", "a_t", "b_t", "k_t", "b_e", "k_e", "v")


def _scan_kernel(zs_ref, w0_ref, wdec_ref, a0_ref, wa_ref, kk_ref, ka_ref, seg_ref,
                 y_ref, state_ref, pre_b_ref, pre_f_ref, res_b_ref, res_f_ref,
                 *, tt, nt, rev):
    step = pl.program_id(0)
    slot_w = step % 2
    slot_r = 1 - slot_w
    n_halves = tt // (2 * CHUNK)

    @pl.when(step == 0)
    def _():
        pre_b_ref[1] = jnp.zeros(pre_b_ref.shape[1:], BF16)
        pre_f_ref[1] = jnp.zeros(pre_f_ref.shape[1:], F32)
        res_b_ref[0] = jnp.zeros(res_b_ref.shape[1:], BF16)
        res_f_ref[0] = jnp.zeros(res_f_ref.shape[1:], F32)
        state_ref[...] = jnp.zeros_like(state_ref)

    trow = lax.broadcasted_iota(jnp.int32, (CHUNK, CHUNK), 0)
    tcol = lax.broadcasted_iota(jnp.int32, (CHUNK, CHUNK), 1)
    tri = jnp.where((tcol >= trow) if rev else (tcol <= trow), 1.0, 0.0).astype(BF16)

    def prep_task(g):
        rows = slice(g * 2 * CHUNK, (g + 1) * 2 * CHUNK)
        lora = zs_ref[rows, 3 * HEADS_W:RWKV_IN]
        xdec = w0_ref[...] + _dot(jnp.tanh(lora).astype(BF16), wdec_ref[...])
        lw = (-math.exp(-0.5)) * _sigmoid(xdec)
        yield
        asig = _sigmoid(a0_ref[...] + _dot(lora.astype(BF16), wa_ref[...]))
        yield
        k = zs_ref[rows, HEADS_W:2 * HEADS_W]
        kk = k * kk_ref[...]
        ssq = _segsum(kk * kk, seg_ref[...])
        yield
        kk = kk / jnp.maximum(jnp.sqrt(ssq), 1e-12)
        kdir = k * (1.0 + (asig - 1.0) * ka_ref[...])
        bvec = kk * asig
        yield
        pack = lambda x: jnp.concatenate([x[:CHUNK], x[CHUNK:]], axis=1)
        lw_p = pack(lw)
        lw3 = _split3(lw_p)
        cum = _dot(tri, lw3[0]) + _dot(tri, lw3[1]) + _dot(tri, lw3[2])
        yield
        last = 0 if rev else CHUNK - 1
        tot = cum[last:last + 1, :]
        e_neg = jnp.exp(-cum)
        e_end = jnp.exp(tot - cum)
        yield
        r_t = pack(zs_ref[rows, 0:HEADS_W]) * jnp.exp(cum)
        pre_f_ref[slot_w, g, 0:CHUNK, :] = r_t
        pre_f_ref[slot_w, g, CHUNK:CHUNK + 1, :] = jnp.exp(tot)
        yield
        kd_p, bv_p = pack(kdir), pack(bvec)
        out = dict(r_t=r_t, a_t=-pack(kk) * jnp.exp(cum - lw_p), b_t=bv_p * e_neg,
                   k_t=kd_p * e_neg, b_e=bv_p * e_end, k_e=kd_p * e_end,
                   v=pack(zs_ref[rows, 2 * HEADS_W:3 * HEADS_W]))
        for i, name in enumerate(_PRE_BF16):
            pre_b_ref[slot_w, g, i] = out[name].astype(BF16)
            yield

    def load_pre(g):
        pre = {name: pre_b_ref[slot_r, g, i] for i, name in enumerate(_PRE_BF16)}
        pre["r_f"] = pre_f_ref[slot_r, g, 0:CHUNK, :]
        pre["p_end"] = pre_f_ref[slot_r, g, CHUNK:CHUNK + 1, :]
        return pre

    gw = 2 * LANES
    head0 = lax.broadcasted_iota(jnp.int32, (1, LANES), 1) < HEAD_DIM
    tpos = lax.broadcasted_iota(jnp.int32, (CHUNK, gw), 0)
    spos = lax.broadcasted_iota(jnp.int32, (CHUNK, gw), 1) % CHUNK
    if rev:
        incl, strict = spos >= tpos, spos > tpos
    else:
        incl, strict = spos <= tpos, spos < tpos
    eye_g = jnp.where(spos == tpos, 1.0, 0.0)
    prow = lax.broadcasted_iota(jnp.int32, (LANES, LANES), 0)
    pcol = lax.broadcasted_iota(jnp.int32, (LANES, LANES), 1)
    pair_diag = (prow // HEAD_DIM) == (pcol // HEAD_DIM)

    def bd(xb):
        lo, hi = xb[:, :LANES], xb[:, LANES:]
        z = jnp.zeros_like(lo)
        return jnp.concatenate([
            jnp.concatenate([jnp.where(head0, lo, z), z], axis=1),
            jnp.concatenate([jnp.where(head0, z, lo), z], axis=1),
            jnp.concatenate([z, jnp.where(head0, hi, z)], axis=1),
            jnp.concatenate([z, jnp.where(head0, z, hi)], axis=1)], axis=0)

    def piece(x, p, cc):
        lane0 = cc * HEADS_W + p * LANES
        return x[:, lane0:lane0 + LANES]

    def grp(x, p):
        return jnp.concatenate([piece(x, p, 0), piece(x, p, 1)], axis=1)

    def sibling(m):
        return jnp.logical_and(tpos // (2 * m) == spos // (2 * m), tpos // m != spos // m)

    pairs = range(N_PAIRS)
    cc_order = (1, 0) if rev else (0, 1)

    def group_task(pre, p, g):
        rg = grp(pre["r_f"], p)
        ag_b = grp(pre["a_t"], p)
        lhs = jnp.concatenate([ag_b, grp(pre["r_t"], p)], axis=0)
        s1 = _dot_tb(lhs, bd(grp(pre["b_t"], p)))
        yield
        s2 = _dot_tb(lhs, bd(grp(pre["k_t"], p)))
        yield
        a_ab = jnp.where(strict, s1[:CHUNK], 0.0)
        a_rb = jnp.where(incl, s1[CHUNK:], 0.0).astype(BF16)
        a_ak = jnp.where(strict, s2[:CHUNK], 0.0).astype(BF16)
        a_rk = jnp.where(incl, s2[CHUNK:], 0.0).astype(BF16)
        tinv = eye_g + jnp.where(sibling(1), a_ab, 0.0)
        m = 2
        while m < CHUNK:
            a_off = jnp.where(sibling(m), a_ab, 0.0).astype(BF16)
            db = tinv.astype(BF16)
            yb = _dot(a_off, bd(db)).astype(BF16)
            yield
            tinv = tinv + _dot(db, bd(yb))
            yield
            m *= 2
        tb = tinv.astype(BF16)
        vbd = bd(grp(pre["v"], p))
        w2 = _dot(a_ak, vbd).astype(BF16)
        yield
        ta = _dot(tb, bd(ag_b))
        yield
        tv = _dot(tb, bd(w2))
        yield
        qg = rg + _dot(a_rb, bd(ta.astype(BF16)))
        yield
        yg = _dot(a_rb, bd(tv.astype(BF16))) + _dot(a_rk, vbd)
        yield
        for cc in cc_order:
            half = slice(cc * LANES, (cc + 1) * LANES)
            be = piece(pre["b_e"], p, cc)
            xt = jnp.where(pair_diag, _dot_ta(ta[:, half].astype(BF16), be), 0.0)
            yield
            ht = jnp.where(pair_diag, _dot_ta(
                jnp.concatenate([tv[:, half].astype(BF16), piece(pre["v"], p, cc)], axis=0),
                jnp.concatenate([be, piece(pre["k_e"], p, cc)], axis=0)), 0.0)
            yield
            i = p * (2 * n_halves) + 2 * g + cc
            res_b_ref[slot_r, i, 0:LANES, :] = xt.astype(BF16)
            res_b_ref[slot_r, i, LANES:LANES + CHUNK, :] = qg[:, half].astype(BF16)
            res_f_ref[slot_r, i, 0:LANES, :] = ht
            res_f_ref[slot_r, i, LANES:LANES + CHUNK, :] = yg[:, half]
            res_f_ref[slot_r, i, LANES + CHUNK:LANES + CHUNK + 1, :] = piece(pre["p_end"], p, cc)

    fresh = (step - 2) % nt == 0
    state = {p: jnp.where(fresh, 0.0, state_ref[p]) for p in pairs}

    def chain_task():
        for g in halves:
            for cc in cc_order:
                c = 2 * g + cc
                sb = {p: state[p].astype(BF16) for p in pairs}
                for p in pairs:
                    i = p * (2 * n_halves) + c
                    y_ref[c * CHUNK:(c + 1) * CHUNK, p * LANES:(p + 1) * LANES] = (
                        res_f_ref[slot_w, i, LANES:LANES + CHUNK, :]
                        + _dot_tb(res_b_ref[slot_w, i, LANES:LANES + CHUNK, :], sb[p]))
                for p in pairs:
                    i = p * (2 * n_halves) + c
                    state[p] = (state[p] * res_f_ref[slot_w, i, LANES + CHUNK:LANES + CHUNK + 1, :]
                                + _dot(sb[p], res_b_ref[slot_w, i, 0:LANES, :])
                                + res_f_ref[slot_w, i, 0:LANES, :])
                yield

    halves = list(range(n_halves))
    if rev:
        halves.reverse()
    tasks = [group_task(load_pre(g), p, g) for g in halves for p in pairs]
    preps = [prep_task(g) for g in halves]
    chain = chain_task()
    rounds = 0
    while tasks:
        tasks = [task for task in tasks if next(task, _DONE) is not _DONE]
        if preps and next(preps[0], _DONE) is _DONE:
            preps.pop(0)
        if rounds % SCAN_CHAIN_EVERY == SCAN_CHAIN_EVERY - 1:
            next(chain, _DONE)
        rounds += 1
    for task in preps + [chain]:
        for _ in task:
            pass
    for p in pairs:
        state_ref[p] = state[p]


def _scan(zs3, w0, wdec, a0, wa, k_k, k_a, seg, b, t, tt, rev):
    nt = t // tt
    total = b * nt
    zs_tiles = zs3.reshape(total, tt, RWKV_IN)

    def tile_of(pos):
        if not rev:
            return pos
        return (pos // nt) * nt + (nt - 1 - pos % nt)

    const = lambda s: (0, 0)
    vec = lambda w: pl.BlockSpec((1, w), const)
    in_specs = [pl.BlockSpec((None, tt, RWKV_IN),
                             lambda s: (tile_of(jnp.minimum(s, total - 1)), 0, 0)),
                vec(HEADS_W), pl.BlockSpec((LORA_W, HEADS_W), const),
                vec(HEADS_W), pl.BlockSpec((LORA_W, HEADS_W), const),
                vec(HEADS_W), vec(HEADS_W), pl.BlockSpec((HEADS_W, HEADS_W), const)]
    n_halves = tt // (2 * CHUNK)
    packed_w = 2 * HEADS_W
    n_pc = N_PAIRS * 2 * n_halves
    y = pl.pallas_call(
        functools.partial(_scan_kernel, tt=tt, nt=nt, rev=rev),
        grid=(total + 2,),
        in_specs=in_specs,
        out_specs=pl.BlockSpec((None, tt, HEADS_W),
                               lambda s: (tile_of(jnp.maximum(s - 2, 0)), 0, 0)),
        out_shape=jax.ShapeDtypeStruct((total, tt, HEADS_W), F32),
        scratch_shapes=[pltpu.VMEM((N_PAIRS, LANES, LANES), F32),
                        pltpu.VMEM((2, n_halves, len(_PRE_BF16), CHUNK, packed_w), BF16),
                        pltpu.VMEM((2, n_halves, CHUNK + HALO_ROWS, packed_w), F32),
                        pltpu.VMEM((2, n_pc, LANES + CHUNK, LANES), BF16),
                        pltpu.VMEM((2, n_pc, LANES + CHUNK + HALO_ROWS, LANES), F32)],
        compiler_params=pltpu.CompilerParams(
            dimension_semantics=("arbitrary",), vmem_limit_bytes=VMEM_LIMIT),
        name="rwkv_scan_bwd" if rev else "rwkv_scan_fwd",
    )(zs_tiles, w0, wdec, a0, wa, k_k, k_a, seg)
    return y.reshape(b * t, HEADS_W)


def _mix_ffn_kernel(x_ref, att_ref, conv_ref, zs_ref, yf_ref, yb_ref,
                    a0_ref, wa_ref, wg_ref, ka_ref, rk_ref, lnw_ref, lnb_ref, seg_ref,
                    ga_ref, gb_ref, wout_ref, gmix_ref, gpre_ref, w1_ref, w2_ref, gffn_ref,
                    o_ref, *, ff_tile):
    seg = seg_ref[...]

    r = zs_ref[:, 0:HEADS_W]
    k = zs_ref[:, HEADS_W:2 * HEADS_W]
    v = zs_ref[:, 2 * HEADS_W:3 * HEADS_W]
    lora = zs_ref[:, 3 * HEADS_W:RWKV_IN]
    lora_b = lora.astype(BF16)
    asum = (_sigmoid(a0_ref[0:1, :] + _dot(lora_b, wa_ref[0]))
            + _sigmoid(a0_ref[1:2, :] + _dot(lora_b, wa_ref[1])))
    gate = _dot(_sigmoid(lora).astype(BF16), wg_ref[...])
    ksum = k * (2.0 + (asum - 2.0) * ka_ref[...])
    bonus = _segsum(r * ksum * rk_ref[...], seg) * v
    y = yf_ref[...] + yb_ref[...]
    yc = y - _segsum(y, seg) * (1.0 / HEAD_DIM)
    var = _segsum(yc * yc, seg) * (1.0 / HEAD_DIM)
    yn = yc * lax.rsqrt(var + LNX_EPS) * lnw_ref[...] + lnb_ref[...]
    y_rwkv = ((yn + bonus) * gate).astype(BF16)

    ya = _rms(att_ref[...], ga_ref[...]).astype(BF16)
    yb = _rms(conv_ref[...], gb_ref[...]).astype(BF16)
    mix = (_dot(ya, wout_ref[0:HEADS_W, :])
           + _dot(yb, wout_ref[HEADS_W:HEADS_W + CONV_W, :])
           + _dot(y_rwkv, wout_ref[HEADS_W + CONV_W:, :]))
    x = x_ref[...] + _rms(mix, gmix_ref[...])

    h = _rms(x, gpre_ref[...]).astype(BF16)
    acc = None
    for j in range(D_FF // ff_tile):
        sl = slice(j * ff_tile, (j + 1) * ff_tile)
        a = jnp.maximum(_dot(h, w1_ref[:, sl]), 0.0)
        part = _dot((a * a).astype(BF16), w2_ref[sl, :])
        acc = part if acc is None else acc + part
    o_ref[...] = x + _rms(acc, gffn_ref[...])


def _mix_ffn(x2, att, conv, zs, yf, yb, lp, tm):
    n = x2.shape[0]
    row = lambda i: (i, 0)
    tile = lambda w: pl.BlockSpec((tm, w), row)
    fixed = lambda *shape: pl.BlockSpec(shape, lambda i: (0,) * len(shape),
                                        pipeline_mode=pl.Buffered(1))
    return pl.pallas_call(
        functools.partial(_mix_ffn_kernel, ff_tile=1024),
        grid=(n // tm,),
        in_specs=[tile(D_MODEL), tile(HEADS_W), tile(CONV_W), tile(RWKV_IN), tile(HEADS_W),
                  tile(HEADS_W),
                  fixed(2, HEADS_W), fixed(2, LORA_W, HEADS_W), fixed(LORA_W, HEADS_W),
                  fixed(1, HEADS_W), fixed(1, HEADS_W), fixed(1, HEADS_W), fixed(1, HEADS_W),
                  fixed(HEADS_W, HEADS_W),
                  fixed(1, HEADS_W), fixed(1, CONV_W), fixed(D_MODEL, D_MODEL),
                  fixed(1, D_MODEL), fixed(1, D_MODEL), fixed(D_MODEL, D_FF),
                  fixed(D_FF, D_MODEL), fixed(1, D_MODEL)],
        out_specs=tile(D_MODEL),
        out_shape=jax.ShapeDtypeStruct((n, D_MODEL), F32),
        compiler_params=pltpu.CompilerParams(
            dimension_semantics=("parallel",), vmem_limit_bytes=VMEM_LIMIT),
        name="mix_ffn",
    )(x2, att, conv, zs, yf, yb,
      lp["a0"], lp["wa"], lp["wg"], lp["k_a"], lp["r_k"], lp["lnx_w"], lp["lnx_b"], lp["seg"],
      lp["attn_g"], lp["conv_g"], lp["w_out"], lp["g_mix_post"], lp["g_ffn_pre"],
      lp["w1"], lp["w2"], lp["g_ffn_post"])


def _pad_rows(w, start):
    return jnp.zeros((LORA_W, HEADS_W), F32).at[start:start + w.shape[0]].set(w)


def _layer_params(l, p):
    row = lambda a: a.reshape(1, -1).astype(F32)
    seg = np.kron(np.eye(N_HEADS, dtype=np.float32), np.ones((HEAD_DIM, HEAD_DIM), np.float32))
    return dict(
        g_mix_pre=row(p["norm_mix_pre"][l]), g_mix_post=row(p["norm_mix_post"][l]),
        g_ffn_pre=row(p["norm_ffn_pre"][l]), g_ffn_post=row(p["norm_ffn_post"][l]),
        w_in=p["w_in"][l].astype(BF16), w_out=p["w_out"][l].astype(BF16),
        attn_g=row(p["attn_out_g"][l]), conv_w=p["conv_w"][l].astype(F32),
        conv_g=row(p["conv_out_g"][l]), mu=row(p["rwkv_mu"][l]),
        w0=p["decay_w0"][l].astype(F32), a0=p["iclr_a0"][l].astype(F32),
        wdec=jnp.stack([_pad_rows(p["decay_up"][l, d], 0) for d in range(2)]).astype(BF16),
        wa=jnp.stack([_pad_rows(p["iclr_up"][l, d], DECAY_RANK) for d in range(2)]).astype(BF16),
        wg=_pad_rows(p["gate_up"][l], DECAY_RANK + ICLR_RANK).astype(BF16),
        k_k=row(p["k_k"][l]), k_a=row(p["k_a"][l]), r_k=row(p["r_k"][l]),
        lnx_w=row(p["lnx_w"][l]), lnx_b=row(p["lnx_b"][l]),
        w1=p["ffn_w1"][l].astype(BF16), w2=p["ffn_w2"][l].astype(BF16),
        seg=jnp.asarray(seg, BF16),
    )


def _layer(x2, lp, biases, b, t):
    n = b * t
    tm = min(512, n)
    tt = min(SCAN_TILE, t)
    zq, conv, zs = _inproj(x2, lp["g_mix_pre"], lp["w_in"], lp["mu"], lp["conv_w"], tm, t)
    att = _attention(zq, biases, b, t)
    zs3 = zs.reshape(b, t, RWKV_IN)
    ys = [_scan(zs3, lp["w0"][d:d + 1], lp["wdec"][d], lp["a0"][d:d + 1], lp["wa"][d],
                lp["k_k"], lp["k_a"], lp["seg"], b, t, tt, rev=bool(d)).reshape(n, HEADS_W)
          for d in range(2)]
    return _mix_ffn(x2, att, conv, zs, ys[0], ys[1], lp, tm)


def _trunk(x, rel_bias, layers):
    b, t, _ = x.shape
    biases = [_attn_bias(rel_bias, t, dil) for dil in DILATIONS]
    x2 = x.reshape(b * t, D_MODEL)
    for lp in layers:
        x2 = _layer(x2, lp, biases, b, t)
    return x2.reshape(b, t, D_MODEL)


def kernel(x_prompt, x_sample, rel_bias, norm_mix_pre, norm_mix_post, norm_ffn_pre, norm_ffn_post, w_in, w_out, attn_out_g, conv_w, conv_out_g, rwkv_mu, decay_w0, decay_up, iclr_a0, iclr_up, gate_up, k_k, k_a, r_k, lnx_w, lnx_b, ffn_w1, ffn_w2):
    p = dict(norm_mix_pre=norm_mix_pre, norm_mix_post=norm_mix_post, norm_ffn_pre=norm_ffn_pre,
             norm_ffn_post=norm_ffn_post, w_in=w_in, w_out=w_out, attn_out_g=attn_out_g,
             conv_w=conv_w, conv_out_g=conv_out_g, rwkv_mu=rwkv_mu, decay_w0=decay_w0,
             decay_up=decay_up, iclr_a0=iclr_a0, iclr_up=iclr_up, gate_up=gate_up, k_k=k_k,
             k_a=k_a, r_k=r_k, lnx_w=lnx_w, lnx_b=lnx_b, ffn_w1=ffn_w1, ffn_w2=ffn_w2)
    layers = [_layer_params(l, p) for l in range(w_in.shape[0])]
    return (_trunk(x_prompt, rel_bias, layers), _trunk(x_sample, rel_bias, layers))
```

```python
import functools
import math

import numpy as np
import jax
import jax.numpy as jnp
from jax import lax
from jax.experimental import pallas as pl
from jax.experimental.pallas import tpu as pltpu

F32 = jnp.float32
BF16 = jnp.bfloat16

D_MODEL = 1024
HEAD_DIM = 64
N_HEADS = 6
HEADS_W = N_HEADS * HEAD_DIM
CONV_W = 256
QKV_W = 3 * HEADS_W
CONVIN_W = 3 * CONV_W
LORA_W = 128
DECAY_RANK = 32
ICLR_RANK = 32
RWKV_IN = 3 * HEADS_W + LORA_W
IN_WIDTH = QKV_W + CONVIN_W + RWKV_IN
D_FF = 4 * D_MODEL
DILATIONS = (1, 4, 16)
KEYS_PER_SIDE = 64
ATTN_UNROLL = 4
SCAN_TILE = 512
SCAN_CHAIN_EVERY = 2
SCAN_PREP_PER_STAGE = 1
_DONE = object()
N_BUCKETS = 32
BUCKET_MAX_DIST = 1024
LOG2E = math.log2(math.e)
RMS_EPS = 1e-6
LNX_EPS = 64e-5
CHUNK = 64
LANES = 128
N_PAIRS = HEADS_W // LANES
HALO_ROWS = 8
VMEM_LIMIT = 56 * 1024 * 1024


def _rms(x, g, eps=RMS_EPS):
    return x * lax.rsqrt(jnp.mean(x * x, axis=-1, keepdims=True) + eps) * g


def _sigmoid(x):
    return 1.0 / (1.0 + jnp.exp(-x))


def _dot(a, b):
    return jnp.dot(a, b, preferred_element_type=F32)


def _dot_tb(a, b):
    return lax.dot_general(a, b, (((1,), (1,)), ((), ())), preferred_element_type=F32)


def _dot_ta(a, b):
    return lax.dot_general(a, b, (((0,), (0,)), ((), ())), preferred_element_type=F32)


def _split2(x):
    hi = x.astype(BF16)
    lo = (x - hi.astype(F32)).astype(BF16)
    return hi, lo


def _segsum(x, pair_ones):
    xb = x.astype(BF16)
    return jnp.concatenate([_dot(xb[:, p * LANES:(p + 1) * LANES], pair_ones)
                            for p in range(x.shape[1] // LANES)], axis=1)


def _neighbours(main, prev_row, next_row):
    tt = main.shape[0]
    row = lax.broadcasted_iota(jnp.int32, main.shape, 0)
    prev = jnp.where(row == 0, prev_row, pltpu.roll(main, 1, 0))
    nxt = jnp.where(row == tt - 1, next_row, pltpu.roll(main, tt - 1, 0))
    return prev, nxt


def _inproj_kernel(x_ref, xp_ref, xn_ref, g_ref, w_ref, mu_ref, cw_ref,
                   zq_ref, conv_ref, zs_ref, *, tm, t):
    i = pl.program_id(0)
    x_ext = jnp.concatenate([x_ref[...], xp_ref[...], xn_ref[...]], axis=0)
    h_ext = _rms(x_ext, g_ref[...]).astype(BF16)
    zq_ref[...] = _dot(h_ext[:tm], w_ref[:, 0:QKV_W])
    z_ext = _dot(h_ext, w_ref[:, QKV_W:IN_WIDTH])
    prev_row = jnp.where((i * tm) % t == 0, 0.0,
                         z_ext[tm + HALO_ROWS - 1:tm + HALO_ROWS, :])
    next_row = jnp.where(((i + 1) * tm) % t == 0, 0.0,
                         z_ext[tm + HALO_ROWS:tm + HALO_ROWS + 1, :])
    z = z_ext[:tm]

    gated = lambda a: a[:, CONV_W:2 * CONV_W] * a[:, 2 * CONV_W:3 * CONV_W]
    u = gated(z)
    u_prev, u_next = _neighbours(u, gated(prev_row), gated(next_row))
    conv_ref[...] = z[:, 0:CONV_W] * (
        u_prev * cw_ref[0:1, :] + u * cw_ref[1:2, :] + u_next * cw_ref[2:3, :])

    zc = z[:, CONVIN_W:]
    c_prev, c_next = _neighbours(zc, prev_row[:, CONVIN_W:], next_row[:, CONVIN_W:])
    zs_ref[...] = zc + (0.5 * (c_prev + c_next) - zc) * mu_ref[...]


def _inproj(x2, g, w_bf16, mu, cw, tm, t):
    n = x2.shape[0]
    per = tm // HALO_ROWS
    last = n // HALO_ROWS - 1
    row = lambda i: (i, 0)
    const = lambda i: (0, 0)
    return pl.pallas_call(
        functools.partial(_inproj_kernel, tm=tm, t=t),
        grid=(n // tm,),
        in_specs=[pl.BlockSpec((tm, D_MODEL), row),
                  pl.BlockSpec((HALO_ROWS, D_MODEL), lambda i: (jnp.maximum(i * per - 1, 0), 0)),
                  pl.BlockSpec((HALO_ROWS, D_MODEL), lambda i: (jnp.minimum((i + 1) * per, last), 0)),
                  pl.BlockSpec((1, D_MODEL), const),
                  pl.BlockSpec((D_MODEL, IN_WIDTH), const),
                  pl.BlockSpec((1, RWKV_IN), const),
                  pl.BlockSpec((3, CONV_W), const)],
        out_specs=[pl.BlockSpec((tm, QKV_W), row),
                   pl.BlockSpec((tm, CONV_W), row),
                   pl.BlockSpec((tm, RWKV_IN), row)],
        out_shape=[jax.ShapeDtypeStruct((n, QKV_W), F32),
                   jax.ShapeDtypeStruct((n, CONV_W), F32),
                   jax.ShapeDtypeStruct((n, RWKV_IN), F32)],
        compiler_params=pltpu.CompilerParams(
            dimension_semantics=("parallel",), vmem_limit_bytes=VMEM_LIMIT),
        name="inproj",
    )(x2, x2, x2, g, w_bf16, mu, cw)


def _t5_bucket(rel):
    half = N_BUCKETS // 2
    max_exact = half // 2
    ret = np.where(rel > 0, half, 0)
    n = np.abs(rel)
    large = max_exact + (np.log(np.maximum(n, 1) / max_exact)
                         / np.log(BUCKET_MAX_DIST / max_exact) * (half - max_exact)).astype(np.int32)
    large = np.minimum(large, half - 1)
    return (ret + np.where(n < max_exact, n, large)).astype(np.int32)


def _attn_cfg(t, dil):
    cls_len = t // dil
    bq = min(128, cls_len)
    wk = min(bq + 2 * KEYS_PER_SIDE, cls_len)
    return cls_len, bq, wk, cls_len // bq


def _attn_bias(rel_bias, t, dil):
    _, bq, wk, _ = _attn_cfg(t, dil)
    out = []
    for shift in (0, KEYS_PER_SIDE, wk - bq):
        rel = np.arange(wk)[None, :] - np.arange(bq)[:, None] - shift
        valid = np.abs(rel) <= KEYS_PER_SIDE
        onehot = jnp.asarray(_t5_bucket(rel * dil))[:, :, None] == jnp.arange(N_BUCKETS)
        table = jnp.transpose(rel_bias.astype(F32))[:, None, None, :]
        bias = jnp.sum(jnp.where(onehot[None], table, 0.0), axis=-1)
        out.append(jnp.where(valid[None], bias * LOG2E, -jnp.inf))
    return jnp.stack(out)


def _attn_kernel(q_ref, k_ref, v_ref, b0_ref, b1_ref, b2_ref, o_ref, m_ref, s_ref, *, t):
    lane = lax.broadcasted_iota(jnp.int32, (1, LANES), 1)
    head0 = lane < HEAD_DIM
    bias_refs = (b0_ref, b1_ref, b2_ref)

    for di, dil in enumerate(DILATIONS):
        cls_len, bq, wk, nb = _attn_cfg(t, dil)
        bias_ref = bias_refs[di]

        def rows(start, size, dil=dil):
            if dil == 1:
                return pl.ds(start, size)
            return pl.ds(start, size, stride=dil)

        def do_blocks(blocks, di=di, dil=dil, bq=bq, wk=wk, cls_len=cls_len,
                      bias_ref=bias_ref, rows=rows):
            qrows, q, kw, vw = [], [], [], []
            for c, blk, placement in blocks:
                m0 = blk * bq
                ws = (0, m0 - KEYS_PER_SIDE, cls_len - wk)[placement]
                qrows.append(rows(c + dil * m0, bq))
                krows = rows(c + dil * ws, wk)
                q.append(q_ref[qrows[-1], :] * (HEAD_DIM ** -0.5 * LOG2E))
                kw.append(k_ref[krows, :].astype(BF16))
                vw.append(v_ref[krows, :].astype(BF16))
            items = [(i, h) for i in range(len(blocks)) for h in range(2)]
            hmask = (head0, jnp.logical_not(head0))
            logits = {(i, h): _dot_tb(jnp.where(hmask[h], q[i], 0.0).astype(BF16), kw[i])
                      + bias_ref[blocks[i][2], h] for i, h in items}
            mh = {k: jnp.max(logits[k], axis=-1, keepdims=True) for k in items}
            p = {k: jnp.exp2(logits[k] - mh[k]) for k in items}
            sh = {k: jnp.sum(p[k], axis=-1, keepdims=True) for k in items}
            oh = {(i, h): _dot(p[(i, h)].astype(BF16), vw[i]) for i, h in items}
            for i in range(len(blocks)):
                m_blk = jnp.where(head0, mh[(i, 0)], mh[(i, 1)])
                s_blk = jnp.where(head0, sh[(i, 0)], sh[(i, 1)])
                o_blk = jnp.where(head0, oh[(i, 0)], oh[(i, 1)])
                if di == 0:
                    m_ref[qrows[i], :] = m_blk
                    s_ref[qrows[i], :] = s_blk
                    o_ref[qrows[i], :] = o_blk
                else:
                    m_old = m_ref[qrows[i], :]
                    m_new = jnp.maximum(m_old, m_blk)
                    a_old = jnp.exp2(m_old - m_new)
                    a_blk = jnp.exp2(m_blk - m_new)
                    m_ref[qrows[i], :] = m_new
                    s_ref[qrows[i], :] = s_ref[qrows[i], :] * a_old + s_blk * a_blk
                    o_ref[qrows[i], :] = o_ref[qrows[i], :] * a_old + o_blk * a_blk

        def loop(n, body):
            if n == 1:
                body(0)
            else:
                lax.fori_loop(0, n, lambda i, carry: (body(i), carry)[1], 0)

        edge = [(0, 0)] + ([(nb - 1, 2)] if nb > 1 else [])
        cb = math.gcd(dil, max(1, ATTN_UNROLL // len(edge)))
        loop(dil // cb, lambda i, edge=edge, cb=cb, do_blocks=do_blocks: do_blocks(
            [(i * cb + j, blk, placement) for j in range(cb) for blk, placement in edge]))
        n_mid = max(nb - 2, 0)
        if n_mid:
            u = max(d for d in range(1, ATTN_UNROLL + 1) if n_mid % d == 0)
            per = n_mid // u
            loop(dil * per, lambda i, u=u, per=per, do_blocks=do_blocks: do_blocks(
                [(i // per, 1 + (i % per) * u + j, 1) for j in range(u)]))

    rb = min(256, t)

    def norm_body(i, carry):
        sl = pl.ds(pl.multiple_of(i * rb, rb), rb)
        o_ref[sl, :] = o_ref[sl, :] / s_ref[sl, :]
        return carry

    lax.fori_loop(0, t // rb, norm_body, 0)


def _attention(zq, biases, b, t):
    zq3 = zq.reshape(b, t, QKV_W)
    in_specs = [pl.BlockSpec((None, t, LANES), lambda i, hp: (i, 0, hp)),
                pl.BlockSpec((None, t, LANES), lambda i, hp: (i, 0, N_PAIRS + hp)),
                pl.BlockSpec((None, t, LANES), lambda i, hp: (i, 0, 2 * N_PAIRS + hp))]
    for bias in biases:
        in_specs.append(pl.BlockSpec((3, 2) + bias.shape[2:], lambda i, hp: (0, hp, 0, 0)))
    out = pl.pallas_call(
        functools.partial(_attn_kernel, t=t),
        grid=(b, N_PAIRS),
        in_specs=in_specs,
        out_specs=pl.BlockSpec((None, t, LANES), lambda i, hp: (i, 0, hp)),
        out_shape=jax.ShapeDtypeStruct((b, t, HEADS_W), F32),
        scratch_shapes=[pltpu.VMEM((t, LANES), F32), pltpu.VMEM((t, LANES), F32)],
        compiler_params=pltpu.CompilerParams(
            dimension_semantics=("parallel", "parallel"), vmem_limit_bytes=VMEM_LIMIT),
        name="dilated_attention",
    )(zq3, zq3, zq3, *biases)
    return out.reshape(b * t, HEADS_W)


_PRE_BF16 = ("r_t", "a_t", "b_t", "k_t", "b_e", "k_e", "v")


def _scan_kernel(zs_ref, w0_ref, wdec_ref, a0_ref, wa_ref, kk_ref, ka_ref, seg_ref,
                 y_ref, state_ref, pre_b_ref, pre_f_ref, res_b_ref, res_f_ref,
                 *, tt, nt, rev):
    step = pl.program_id(0)
    slot_w = step % 2
    slot_r = 1 - slot_w
    n_halves = tt // (2 * CHUNK)

    @pl.when(step == 0)
    def _():
        pre_b_ref[1] = jnp.zeros(pre_b_ref.shape[1:], BF16)
        pre_f_ref[1] = jnp.zeros(pre_f_ref.shape[1:], F32)
        res_b_ref[0] = jnp.zeros(res_b_ref.shape[1:], BF16)
        res_f_ref[0] = jnp.zeros(res_f_ref.shape[1:], F32)
        state_ref[...] = jnp.zeros_like(state_ref)

    trow = lax.broadcasted_iota(jnp.int32, (CHUNK, CHUNK), 0)
    tcol = lax.broadcasted_iota(jnp.int32, (CHUNK, CHUNK), 1)
    tri = jnp.where((tcol >= trow) if rev else (tcol <= trow), 1.0, 0.0).astype(BF16)

    def prep_task(g):
        rows = slice(g * 2 * CHUNK, (g + 1) * 2 * CHUNK)
        lora = zs_ref[rows, 3 * HEADS_W:RWKV_IN]
        xdec = w0_ref[...] + _dot(jnp.tanh(lora).astype(BF16), wdec_ref[...])
        lw = (-math.exp(-0.5)) * _sigmoid(xdec)
        yield
        asig = _sigmoid(a0_ref[...] + _dot(lora.astype(BF16), wa_ref[...]))
        yield
        k = zs_ref[rows, HEADS_W:2 * HEADS_W]
        kk = k * kk_ref[...]
        ssq = _segsum(kk * kk, seg_ref[...])
        yield
        kk = kk / jnp.maximum(jnp.sqrt(ssq), 1e-12)
        kdir = k * (1.0 + (asig - 1.0) * ka_ref[...])
        bvec = kk * asig
        yield
        pack = lambda x: jnp.concatenate([x[:CHUNK], x[CHUNK:]], axis=1)
        lw_p = pack(lw)
        lw_hi, lw_lo = _split2(lw_p)
        cum = _dot(tri, lw_hi) + _dot(tri, lw_lo)
        yield
        last = 0 if rev else CHUNK - 1
        tot = cum[last:last + 1, :]
        e_neg = jnp.exp(-cum)
        e_end = jnp.exp(tot - cum)
        yield
        r_t = pack(zs_ref[rows, 0:HEADS_W]) * jnp.exp(cum)
        pre_f_ref[slot_w, g, 0:CHUNK, :] = r_t
        pre_f_ref[slot_w, g, CHUNK:CHUNK + 1, :] = jnp.exp(tot)
        yield
        kd_p, bv_p = pack(kdir), pack(bvec)
        out = dict(r_t=r_t, a_t=-pack(kk) * jnp.exp(cum - lw_p), b_t=bv_p * e_neg,
                   k_t=kd_p * e_neg, b_e=bv_p * e_end, k_e=kd_p * e_end,
                   v=pack(zs_ref[rows, 2 * HEADS_W:3 * HEADS_W]))
        for i, name in enumerate(_PRE_BF16):
            pre_b_ref[slot_w, g, i] = out[name].astype(BF16)
            yield

    def load_pre(g):
        pre = {name: pre_b_ref[slot_r, g, i] for i, name in enumerate(_PRE_BF16)}
        pre["r_f"] = pre_f_ref[slot_r, g, 0:CHUNK, :]
        pre["p_end"] = pre_f_ref[slot_r, g, CHUNK:CHUNK + 1, :]
        return pre

    gw = 2 * LANES
    head0 = lax.broadcasted_iota(jnp.int32, (1, LANES), 1) < HEAD_DIM
    tpos = lax.broadcasted_iota(jnp.int32, (CHUNK, gw), 0)
    spos = lax.broadcasted_iota(jnp.int32, (CHUNK, gw), 1) % CHUNK
    if rev:
        incl, strict = spos >= tpos, spos > tpos
    else:
        incl, strict = spos <= tpos, spos < tpos
    eye_g = jnp.where(spos == tpos, 1.0, 0.0)
    prow = lax.broadcasted_iota(jnp.int32, (LANES, LANES), 0)
    pcol = lax.broadcasted_iota(jnp.int32, (LANES, LANES), 1)
    pair_diag = (prow // HEAD_DIM) == (pcol // HEAD_DIM)

    def bd(xb):
        lo, hi = xb[:, :LANES], xb[:, LANES:]
        z = jnp.zeros_like(lo)
        return jnp.concatenate([
            jnp.concatenate([jnp.where(head0, lo, z), z], axis=1),
            jnp.concatenate([jnp.where(head0, z, lo), z], axis=1),
            jnp.concatenate([z, jnp.where(head0, hi, z)], axis=1),
            jnp.concatenate([z, jnp.where(head0, z, hi)], axis=1)], axis=0)

    def piece(x, p, cc):
        lane0 = cc * HEADS_W + p * LANES
        return x[:, lane0:lane0 + LANES]

    def grp(x, p):
        return jnp.concatenate([piece(x, p, 0), piece(x, p, 1)], axis=1)

    def sibling(m):
        return jnp.logical_and(tpos // (2 * m) == spos // (2 * m), tpos // m != spos // m)

    pairs = range(N_PAIRS)
    cc_order = (1, 0) if rev else (0, 1)

    def group_task(pre, p, g):
        rg = grp(pre["r_f"], p)
        ag_b = grp(pre["a_t"], p)
        lhs = jnp.concatenate([ag_b, grp(pre["r_t"], p)], axis=0)
        s1 = _dot_tb(lhs, bd(grp(pre["b_t"], p)))
        yield
        s2 = _dot_tb(lhs, bd(grp(pre["k_t"], p)))
        yield
        a_ab = jnp.where(strict, s1[:CHUNK], 0.0)
        a_rb = jnp.where(incl, s1[CHUNK:], 0.0).astype(BF16)
        a_ak = jnp.where(strict, s2[:CHUNK], 0.0).astype(BF16)
        a_rk = jnp.where(incl, s2[CHUNK:], 0.0).astype(BF16)
        tinv = eye_g + jnp.where(sibling(1), a_ab, 0.0)
        m = 2
        while m < CHUNK:
            a_off = jnp.where(sibling(m), a_ab, 0.0).astype(BF16)
            db = tinv.astype(BF16)
            yb = _dot(a_off, bd(db)).astype(BF16)
            yield
            tinv = tinv + _dot(db, bd(yb))
            yield
            m *= 2
        tb = tinv.astype(BF16)
        vbd = bd(grp(pre["v"], p))
        w2 = _dot(a_ak, vbd).astype(BF16)
        yield
        ta = _dot(tb, bd(ag_b))
        yield
        tv = _dot(tb, bd(w2))
        yield
        qg = rg + _dot(a_rb, bd(ta.astype(BF16)))
        yield
        yg = _dot(a_rb, bd(tv.astype(BF16))) + _dot(a_rk, vbd)
        yield
        for cc in cc_order:
            half = slice(cc * LANES, (cc + 1) * LANES)
            be = piece(pre["b_e"], p, cc)
            xt = jnp.where(pair_diag, _dot_ta(ta[:, half].astype(BF16), be), 0.0)
            yield
            ht = jnp.where(pair_diag, _dot_ta(
                jnp.concatenate([tv[:, half].astype(BF16), piece(pre["v"], p, cc)], axis=0),
                jnp.concatenate([be, piece(pre["k_e"], p, cc)], axis=0)), 0.0)
            yield
            i = p * (2 * n_halves) + 2 * g + cc
            res_b_ref[slot_r, i, 0:LANES, :] = xt.astype(BF16)
            res_b_ref[slot_r, i, LANES:LANES + CHUNK, :] = qg[:, half].astype(BF16)
            res_f_ref[slot_r, i, 0:LANES, :] = ht
            res_f_ref[slot_r, i, LANES:LANES + CHUNK, :] = yg[:, half]
            res_f_ref[slot_r, i, LANES + CHUNK:LANES + CHUNK + 1, :] = piece(pre["p_end"], p, cc)

    fresh = (step - 2) % nt == 0
    state = {p: jnp.where(fresh, 0.0, state_ref[p]) for p in pairs}

    def chain_task():
        for g in halves:
            for cc in cc_order:
                c = 2 * g + cc
                sb = {p: state[p].astype(BF16) for p in pairs}
                for p in pairs:
                    i = p * (2 * n_halves) + c
                    y_ref[c * CHUNK:(c + 1) * CHUNK, p * LANES:(p + 1) * LANES] = (
                        res_f_ref[slot_w, i, LANES:LANES + CHUNK, :]
                        + _dot_tb(res_b_ref[slot_w, i, LANES:LANES + CHUNK, :], sb[p]))
                for p in pairs:
                    i = p * (2 * n_halves) + c
                    state[p] = (state[p] * res_f_ref[slot_w, i, LANES + CHUNK:LANES + CHUNK + 1, :]
                                + _dot(sb[p], res_b_ref[slot_w, i, 0:LANES, :])
                                + res_f_ref[slot_w, i, 0:LANES, :])
                yield

    halves = list(range(n_halves))
    if rev:
        halves.reverse()
    tasks = [group_task(load_pre(g), p, g) for g in halves for p in pairs]
    preps = [prep_task(g) for g in halves]
    chain = chain_task()
    rounds = 0
    while tasks:
        tasks = [task for task in tasks if next(task, _DONE) is not _DONE]
        for _ in range(SCAN_PREP_PER_STAGE):
            if preps and next(preps[0], _DONE) is _DONE:
                preps.pop(0)
        if rounds % SCAN_CHAIN_EVERY == SCAN_CHAIN_EVERY - 1:
            next(chain, _DONE)
        rounds += 1
    for task in preps + [chain]:
        for _ in task:
            pass
    for p in pairs:
        state_ref[p] = state[p]


def _scan(zs3, w0, wdec, a0, wa, k_k, k_a, seg, b, t, tt, rev):
    nt = t // tt
    total = b * nt
    zs_tiles = zs3.reshape(total, tt, RWKV_IN)

    def tile_of(pos):
        if not rev:
            return pos
        return (pos // nt) * nt + (nt - 1 - pos % nt)

    const = lambda s: (0, 0)
    vec = lambda w: pl.BlockSpec((1, w), const)
    in_specs = [pl.BlockSpec((None, tt, RWKV_IN),
                             lambda s: (tile_of(jnp.minimum(s, total - 1)), 0, 0)),
                vec(HEADS_W), pl.BlockSpec((LORA_W, HEADS_W), const),
                vec(HEADS_W), pl.BlockSpec((LORA_W, HEADS_W), const),
                vec(HEADS_W), vec(HEADS_W), pl.BlockSpec((LANES, LANES), const)]
    n_halves = tt // (2 * CHUNK)
    packed_w = 2 * HEADS_W
    n_pc = N_PAIRS * 2 * n_halves
    y = pl.pallas_call(
        functools.partial(_scan_kernel, tt=tt, nt=nt, rev=rev),
        grid=(total + 2,),
        in_specs=in_specs,
        out_specs=pl.BlockSpec((None, tt, HEADS_W),
                               lambda s: (tile_of(jnp.maximum(s - 2, 0)), 0, 0)),
        out_shape=jax.ShapeDtypeStruct((total, tt, HEADS_W), F32),
        scratch_shapes=[pltpu.VMEM((N_PAIRS, LANES, LANES), F32),
                        pltpu.VMEM((2, n_halves, len(_PRE_BF16), CHUNK, packed_w), BF16),
                        pltpu.VMEM((2, n_halves, CHUNK + HALO_ROWS, packed_w), F32),
                        pltpu.VMEM((2, n_pc, LANES + CHUNK, LANES), BF16),
                        pltpu.VMEM((2, n_pc, LANES + CHUNK + HALO_ROWS, LANES), F32)],
        compiler_params=pltpu.CompilerParams(
            dimension_semantics=("arbitrary",), vmem_limit_bytes=VMEM_LIMIT),
        name="rwkv_scan_bwd" if rev else "rwkv_scan_fwd",
    )(zs_tiles, w0, wdec, a0, wa, k_k, k_a, seg)
    return y.reshape(b * t, HEADS_W)


def _mix_ffn_kernel(x_ref, att_ref, conv_ref, zs_ref, yf_ref, yb_ref,
                    a0_ref, wa_ref, wg_ref, ka_ref, rk_ref, lnw_ref, lnb_ref, seg_ref,
                    ga_ref, gb_ref, wout_ref, gmix_ref, gpre_ref, w1_ref, w2_ref, gffn_ref,
                    o_ref, *, ff_tile):
    seg = seg_ref[...]

    r = zs_ref[:, 0:HEADS_W]
    k = zs_ref[:, HEADS_W:2 * HEADS_W]
    v = zs_ref[:, 2 * HEADS_W:3 * HEADS_W]
    lora = zs_ref[:, 3 * HEADS_W:RWKV_IN]
    lora_b = lora.astype(BF16)
    asum = (_sigmoid(a0_ref[0:1, :] + _dot(lora_b, wa_ref[0]))
            + _sigmoid(a0_ref[1:2, :] + _dot(lora_b, wa_ref[1])))
    gate = _dot(_sigmoid(lora).astype(BF16), wg_ref[...])
    ksum = k * (2.0 + (asum - 2.0) * ka_ref[...])
    bonus = _segsum(r * ksum * rk_ref[...], seg) * v
    y = yf_ref[...] + yb_ref[...]
    yc = y - _segsum(y, seg) * (1.0 / HEAD_DIM)
    var = _segsum(yc * yc, seg) * (1.0 / HEAD_DIM)
    yn = yc * lax.rsqrt(var + LNX_EPS) * lnw_ref[...] + lnb_ref[...]
    y_rwkv = ((yn + bonus) * gate).astype(BF16)

    ya = _rms(att_ref[...], ga_ref[...]).astype(BF16)
    yb = _rms(conv_ref[...], gb_ref[...]).astype(BF16)
    mix = (_dot(ya, wout_ref[0:HEADS_W, :])
           + _dot(yb, wout_ref[HEADS_W:HEADS_W + CONV_W, :])
           + _dot(y_rwkv, wout_ref[HEADS_W + CONV_W:, :]))
    x = x_ref[...] + _rms(mix, gmix_ref[...])

    h = _rms(x, gpre_ref[...]).astype(BF16)
    acc = None
    for j in range(D_FF // ff_tile):
        sl = slice(j * ff_tile, (j + 1) * ff_tile)
        a = jnp.maximum(_dot(h, w1_ref[:, sl]), 0.0)
        part = _dot((a * a).astype(BF16), w2_ref[sl, :])
        acc = part if acc is None else acc + part
    o_ref[...] = x + _rms(acc, gffn_ref[...])


def _mix_ffn(x2, att, conv, zs, yf, yb, lp, tm):
    n = x2.shape[0]
    row = lambda i: (i, 0)
    tile = lambda w: pl.BlockSpec((tm, w), row)
    fixed = lambda *shape: pl.BlockSpec(shape, lambda i: (0,) * len(shape),
                                        pipeline_mode=pl.Buffered(1))
    return pl.pallas_call(
        functools.partial(_mix_ffn_kernel, ff_tile=1024),
        grid=(n // tm,),
        in_specs=[tile(D_MODEL), tile(HEADS_W), tile(CONV_W), tile(RWKV_IN), tile(HEADS_W),
                  tile(HEADS_W),
                  fixed(2, HEADS_W), fixed(2, LORA_W, HEADS_W), fixed(LORA_W, HEADS_W),
                  fixed(1, HEADS_W), fixed(1, HEADS_W), fixed(1, HEADS_W), fixed(1, HEADS_W),
                  fixed(LANES, LANES),
                  fixed(1, HEADS_W), fixed(1, CONV_W), fixed(D_MODEL, D_MODEL),
                  fixed(1, D_MODEL), fixed(1, D_MODEL), fixed(D_MODEL, D_FF),
                  fixed(D_FF, D_MODEL), fixed(1, D_MODEL)],
        out_specs=tile(D_MODEL),
        out_shape=jax.ShapeDtypeStruct((n, D_MODEL), F32),
        compiler_params=pltpu.CompilerParams(
            dimension_semantics=("parallel",), vmem_limit_bytes=VMEM_LIMIT),
        name="mix_ffn",
    )(x2, att, conv, zs, yf, yb,
      lp["a0"], lp["wa"], lp["wg"], lp["k_a"], lp["r_k"], lp["lnx_w"], lp["lnx_b"], lp["seg"],
      lp["attn_g"], lp["conv_g"], lp["w_out"], lp["g_mix_post"], lp["g_ffn_pre"],
      lp["w1"], lp["w2"], lp["g_ffn_post"])


def _pad_rows(w, start):
    return jnp.zeros((LORA_W, HEADS_W), F32).at[start:start + w.shape[0]].set(w)


def _layer_params(l, p):
    row = lambda a: a.reshape(1, -1).astype(F32)
    seg = np.kron(np.eye(LANES // HEAD_DIM, dtype=np.float32),
                  np.ones((HEAD_DIM, HEAD_DIM), np.float32))
    return dict(
        g_mix_pre=row(p["norm_mix_pre"][l]), g_mix_post=row(p["norm_mix_post"][l]),
        g_ffn_pre=row(p["norm_ffn_pre"][l]), g_ffn_post=row(p["norm_ffn_post"][l]),
        w_in=p["w_in"][l].astype(BF16), w_out=p["w_out"][l].astype(BF16),
        attn_g=row(p["attn_out_g"][l]), conv_w=p["conv_w"][l].astype(F32),
        conv_g=row(p["conv_out_g"][l]), mu=row(p["rwkv_mu"][l]),
        w0=p["decay_w0"][l].astype(F32), a0=p["iclr_a0"][l].astype(F32),
        wdec=jnp.stack([_pad_rows(p["decay_up"][l, d], 0) for d in range(2)]).astype(BF16),
        wa=jnp.stack([_pad_rows(p["iclr_up"][l, d], DECAY_RANK) for d in range(2)]).astype(BF16),
        wg=_pad_rows(p["gate_up"][l], DECAY_RANK + ICLR_RANK).astype(BF16),
        k_k=row(p["k_k"][l]), k_a=row(p["k_a"][l]), r_k=row(p["r_k"][l]),
        lnx_w=row(p["lnx_w"][l]), lnx_b=row(p["lnx_b"][l]),
        w1=p["ffn_w1"][l].astype(BF16), w2=p["ffn_w2"][l].astype(BF16),
        seg=jnp.asarray(seg, BF16),
    )


def _layer(x2, lp, biases, b, t):
    n = b * t
    tm = min(512, n)
    tt = min(SCAN_TILE, t)
    zq, conv, zs = _inproj(x2, lp["g_mix_pre"], lp["w_in"], lp["mu"], lp["conv_w"], tm, t)
    att = _attention(zq, biases, b, t)
    zs3 = zs.reshape(b, t, RWKV_IN)
    ys = [_scan(zs3, lp["w0"][d:d + 1], lp["wdec"][d], lp["a0"][d:d + 1], lp["wa"][d],
                lp["k_k"], lp["k_a"], lp["seg"], b, t, tt, rev=bool(d)).reshape(n, HEADS_W)
          for d in range(2)]
    return _mix_ffn(x2, att, conv, zs, ys[0], ys[1], lp, tm)


def _trunk(x, rel_bias, layers):
    b, t, _ = x.shape
    biases = [_attn_bias(rel_bias, t, dil) for dil in DILATIONS]
    x2 = x.reshape(b * t, D_MODEL)
    for lp in layers:
        x2 = _layer(x2, lp, biases, b, t)
    return x2.reshape(b, t, D_MODEL)


def kernel(x_prompt, x_sample, rel_bias, norm_mix_pre, norm_mix_post, norm_ffn_pre, norm_ffn_post, w_in, w_out, attn_out_g, conv_w, conv_out_g, rwkv_mu, decay_w0, decay_up, iclr_a0, iclr_up, gate_up, k_k, k_a, r_k, lnx_w, lnx_b, ffn_w1, ffn_w2):
    p = dict(norm_mix_pre=norm_mix_pre, norm_mix_post=norm_mix_post, norm_ffn_pre=norm_ffn_pre,
             norm_ffn_post=norm_ffn_post, w_in=w_in, w_out=w_out, attn_out_g=attn_out_g,
             conv_w=conv_w, conv_out_g=conv_out_g, rwkv_mu=rwkv_mu, decay_w0=decay_w0,
             decay_up=decay_up, iclr_a0=iclr_a0, iclr_up=iclr_up, gate_up=gate_up, k_k=k_k,
             k_a=k_a, r_k=r_k, lnx_w=lnx_w, lnx_b=lnx_b, ffn_w1=ffn_w1, ffn_w2=ffn_w2)
    layers = [_layer_params(l, p) for l in range(w_in.shape[0])]
    return (_trunk(x_prompt, rel_bias, layers), _trunk(x_sample, rel_bias, layers))
```

```python
import functools
import math

import numpy as np
import jax
import jax.numpy as jnp
from jax import lax
from jax.experimental import pallas as pl
from jax.experimental.pallas import tpu as pltpu

F32 = jnp.float32
BF16 = jnp.bfloat16

D_MODEL = 1024
HEAD_DIM = 64
N_HEADS = 6
HEADS_W = N_HEADS * HEAD_DIM
CONV_W = 256
QKV_W = 3 * HEADS_W
CONVIN_W = 3 * CONV_W
LORA_W = 128
DECAY_RANK = 32
ICLR_RANK = 32
RWKV_IN = 3 * HEADS_W + LORA_W
IN_WIDTH = QKV_W + CONVIN_W + RWKV_IN
D_FF = 4 * D_MODEL
DILATIONS = (1, 4, 16)
KEYS_PER_SIDE = 64
ATTN_UNROLL = 8
SCAN_TILE = 512
SCAN_CHAIN_EVERY = 2
SCAN_PREP_PER_STAGE = 1
_DONE = object()
N_BUCKETS = 32
BUCKET_MAX_DIST = 1024
LOG2E = math.log2(math.e)
RMS_EPS = 1e-6
LNX_EPS = 64e-5
CHUNK = 64
LANES = 128
N_PAIRS = HEADS_W // LANES
HALO_ROWS = 8
VMEM_LIMIT = 56 * 1024 * 1024


def _rms(x, g, eps=RMS_EPS):
    return x * lax.rsqrt(jnp.mean(x * x, axis=-1, keepdims=True) + eps) * g


def _sigmoid(x):
    return 1.0 / (1.0 + jnp.exp(-x))


def _dot(a, b):
    return jnp.dot(a, b, preferred_element_type=F32)


def _dot_tb(a, b):
    return lax.dot_general(a, b, (((1,), (1,)), ((), ())), preferred_element_type=F32)


def _dot_ta(a, b):
    return lax.dot_general(a, b, (((0,), (0,)), ((), ())), preferred_element_type=F32)


def _split2(x):
    hi = x.astype(BF16)
    lo = (x - hi.astype(F32)).astype(BF16)
    return hi, lo


def _segsum(x, pair_ones):
    xb = x.astype(BF16)
    return jnp.concatenate([_dot(xb[:, p * LANES:(p + 1) * LANES], pair_ones)
                            for p in range(x.shape[1] // LANES)], axis=1)


def _neighbours(main, prev_row, next_row):
    tt = main.shape[0]
    row = lax.broadcasted_iota(jnp.int32, main.shape, 0)
    prev = jnp.where(row == 0, prev_row, pltpu.roll(main, 1, 0))
    nxt = jnp.where(row == tt - 1, next_row, pltpu.roll(main, tt - 1, 0))
    return prev, nxt


def _inproj_kernel(x_ref, xp_ref, xn_ref, g_ref, w_ref, mu_ref, cw_ref,
                   zq_ref, conv_ref, zs_ref, *, tm, t):
    i = pl.program_id(0)
    x_ext = jnp.concatenate([x_ref[...], xp_ref[...], xn_ref[...]], axis=0)
    h_ext = _rms(x_ext, g_ref[...]).astype(BF16)
    zq_ref[...] = _dot(h_ext[:tm], w_ref[:, 0:QKV_W])
    z_ext = _dot(h_ext, w_ref[:, QKV_W:IN_WIDTH])
    prev_row = jnp.where((i * tm) % t == 0, 0.0,
                         z_ext[tm + HALO_ROWS - 1:tm + HALO_ROWS, :])
    next_row = jnp.where(((i + 1) * tm) % t == 0, 0.0,
                         z_ext[tm + HALO_ROWS:tm + HALO_ROWS + 1, :])
    z = z_ext[:tm]

    gated = lambda a: a[:, CONV_W:2 * CONV_W] * a[:, 2 * CONV_W:3 * CONV_W]
    u = gated(z)
    u_prev, u_next = _neighbours(u, gated(prev_row), gated(next_row))
    conv_ref[...] = z[:, 0:CONV_W] * (
        u_prev * cw_ref[0:1, :] + u * cw_ref[1:2, :] + u_next * cw_ref[2:3, :])

    zc = z[:, CONVIN_W:]
    c_prev, c_next = _neighbours(zc, prev_row[:, CONVIN_W:], next_row[:, CONVIN_W:])
    zs_ref[...] = zc + (0.5 * (c_prev + c_next) - zc) * mu_ref[...]


def _inproj(x2, g, w_bf16, mu, cw, tm, t):
    n = x2.shape[0]
    per = tm // HALO_ROWS
    last = n // HALO_ROWS - 1
    row = lambda i: (i, 0)
    const = lambda i: (0, 0)
    return pl.pallas_call(
        functools.partial(_inproj_kernel, tm=tm, t=t),
        grid=(n // tm,),
        in_specs=[pl.BlockSpec((tm, D_MODEL), row),
                  pl.BlockSpec((HALO_ROWS, D_MODEL), lambda i: (jnp.maximum(i * per - 1, 0), 0)),
                  pl.BlockSpec((HALO_ROWS, D_MODEL), lambda i: (jnp.minimum((i + 1) * per, last), 0)),
                  pl.BlockSpec((1, D_MODEL), const),
                  pl.BlockSpec((D_MODEL, IN_WIDTH), const),
                  pl.BlockSpec((1, RWKV_IN), const),
                  pl.BlockSpec((3, CONV_W), const)],
        out_specs=[pl.BlockSpec((tm, QKV_W), row),
                   pl.BlockSpec((tm, CONV_W), row),
                   pl.BlockSpec((tm, RWKV_IN), row)],
        out_shape=[jax.ShapeDtypeStruct((n, QKV_W), F32),
                   jax.ShapeDtypeStruct((n, CONV_W), F32),
                   jax.ShapeDtypeStruct((n, RWKV_IN), F32)],
        compiler_params=pltpu.CompilerParams(
            dimension_semantics=("parallel",), vmem_limit_bytes=VMEM_LIMIT),
        name="inproj",
    )(x2, x2, x2, g, w_bf16, mu, cw)


def _t5_bucket(rel):
    half = N_BUCKETS // 2
    max_exact = half // 2
    ret = np.where(rel > 0, half, 0)
    n = np.abs(rel)
    large = max_exact + (np.log(np.maximum(n, 1) / max_exact)
                         / np.log(BUCKET_MAX_DIST / max_exact) * (half - max_exact)).astype(np.int32)
    large = np.minimum(large, half - 1)
    return (ret + np.where(n < max_exact, n, large)).astype(np.int32)


def _attn_cfg(t, dil):
    cls_len = t // dil
    bq = min(128, cls_len)
    wk = min(bq + 2 * KEYS_PER_SIDE, cls_len)
    return cls_len, bq, wk, cls_len // bq


def _attn_bias(rel_bias, t, dil):
    _, bq, wk, _ = _attn_cfg(t, dil)
    out = []
    for shift in (0, KEYS_PER_SIDE, wk - bq):
        rel = np.arange(wk)[None, :] - np.arange(bq)[:, None] - shift
        valid = np.abs(rel) <= KEYS_PER_SIDE
        onehot = jnp.asarray(_t5_bucket(rel * dil))[:, :, None] == jnp.arange(N_BUCKETS)
        table = jnp.transpose(rel_bias.astype(F32))[:, None, None, :]
        bias = jnp.sum(jnp.where(onehot[None], table, 0.0), axis=-1)
        out.append(jnp.where(valid[None], bias * LOG2E, -jnp.inf))
    return jnp.stack(out)


def _attn_kernel(q_ref, k_ref, v_ref, b0_ref, b1_ref, b2_ref, o_ref, m_ref, s_ref, *, t):
    lane = lax.broadcasted_iota(jnp.int32, (1, LANES), 1)
    head0 = lane < HEAD_DIM
    bias_refs = (b0_ref, b1_ref, b2_ref)

    for di, dil in enumerate(DILATIONS):
        cls_len, bq, wk, nb = _attn_cfg(t, dil)
        bias_ref = bias_refs[di]

        def rows(start, size, dil=dil):
            if dil == 1:
                return pl.ds(start, size)
            return pl.ds(start, size, stride=dil)

        def do_blocks(blocks, di=di, dil=dil, bq=bq, wk=wk, cls_len=cls_len,
                      bias_ref=bias_ref, rows=rows):
            qrows, q, kw, vw = [], [], [], []
            for c, blk, placement in blocks:
                m0 = blk * bq
                ws = (0, m0 - KEYS_PER_SIDE, cls_len - wk)[placement]
                qrows.append(rows(c + dil * m0, bq))
                krows = rows(c + dil * ws, wk)
                q.append(q_ref[qrows[-1], :] * (HEAD_DIM ** -0.5 * LOG2E))
                kw.append(k_ref[krows, :].astype(BF16))
                vw.append(v_ref[krows, :].astype(BF16))
            items = [(i, h) for i in range(len(blocks)) for h in range(2)]
            hmask = (head0, jnp.logical_not(head0))
            logits = {(i, h): _dot_tb(jnp.where(hmask[h], q[i], 0.0).astype(BF16), kw[i])
                      + bias_ref[blocks[i][2], h] for i, h in items}
            mh = {k: jnp.max(logits[k], axis=-1, keepdims=True) for k in items}
            p = {k: jnp.exp2(logits[k] - mh[k]) for k in items}
            sh = {k: jnp.sum(p[k], axis=-1, keepdims=True) for k in items}
            oh = {(i, h): _dot(p[(i, h)].astype(BF16), vw[i]) for i, h in items}
            for i in range(len(blocks)):
                m_blk = jnp.where(head0, mh[(i, 0)], mh[(i, 1)])
                s_blk = jnp.where(head0, sh[(i, 0)], sh[(i, 1)])
                o_blk = jnp.where(head0, oh[(i, 0)], oh[(i, 1)])
                if di == 0:
                    m_ref[qrows[i], :] = m_blk
                    s_ref[qrows[i], :] = s_blk
                    o_ref[qrows[i], :] = o_blk
                else:
                    m_old = m_ref[qrows[i], :]
                    m_new = jnp.maximum(m_old, m_blk)
                    a_old = jnp.exp2(m_old - m_new)
                    a_blk = jnp.exp2(m_blk - m_new)
                    m_ref[qrows[i], :] = m_new
                    s_ref[qrows[i], :] = s_ref[qrows[i], :] * a_old + s_blk * a_blk
                    o_ref[qrows[i], :] = o_ref[qrows[i], :] * a_old + o_blk * a_blk

        def loop(n, body):
            if n == 1:
                body(0)
            else:
                lax.fori_loop(0, n, lambda i, carry: (body(i), carry)[1], 0)

        edge = [(0, 0)] + ([(nb - 1, 2)] if nb > 1 else [])
        cb = math.gcd(dil, max(1, ATTN_UNROLL // len(edge)))
        loop(dil // cb, lambda i, edge=edge, cb=cb, do_blocks=do_blocks: do_blocks(
            [(i * cb + j, blk, placement) for j in range(cb) for blk, placement in edge]))
        n_mid = max(nb - 2, 0)
        if n_mid:
            u = max(d for d in range(1, ATTN_UNROLL + 1) if n_mid % d == 0)
            per = n_mid // u
            loop(dil * per, lambda i, u=u, per=per, do_blocks=do_blocks: do_blocks(
                [(i // per, 1 + (i % per) * u + j, 1) for j in range(u)]))

    rb = min(256, t)

    def norm_body(i, carry):
        sl = pl.ds(pl.multiple_of(i * rb, rb), rb)
        o_ref[sl, :] = o_ref[sl, :] / s_ref[sl, :]
        return carry

    lax.fori_loop(0, t // rb, norm_body, 0)


def _attention(zq, biases, b, t):
    zq3 = zq.reshape(b, t, QKV_W)
    in_specs = [pl.BlockSpec((None, t, LANES), lambda i, hp: (i, 0, hp)),
                pl.BlockSpec((None, t, LANES), lambda i, hp: (i, 0, N_PAIRS + hp)),
                pl.BlockSpec((None, t, LANES), lambda i, hp: (i, 0, 2 * N_PAIRS + hp))]
    for bias in biases:
        in_specs.append(pl.BlockSpec((3, 2) + bias.shape[2:], lambda i, hp: (0, hp, 0, 0)))
    out = pl.pallas_call(
        functools.partial(_attn_kernel, t=t),
        grid=(b, N_PAIRS),
        in_specs=in_specs,
        out_specs=pl.BlockSpec((None, t, LANES), lambda i, hp: (i, 0, hp)),
        out_shape=jax.ShapeDtypeStruct((b, t, HEADS_W), F32),
        scratch_shapes=[pltpu.VMEM((t, LANES), F32), pltpu.VMEM((t, LANES), F32)],
        compiler_params=pltpu.CompilerParams(
            dimension_semantics=("parallel", "parallel"), vmem_limit_bytes=VMEM_LIMIT),
        name="dilated_attention",
    )(zq3, zq3, zq3, *biases)
    return out.reshape(b * t, HEADS_W)


_PRE_BF16 = ("r_t", "a_t", "b_t", "k_t", "b_e", "k_e", "v")


def _scan_kernel(zs_ref, w0_ref, wdec_ref, a0_ref, wa_ref, kk_ref, ka_ref, seg_ref,
                 y_ref, state_ref, pre_b_ref, pre_f_ref, res_b_ref, res_f_ref,
                 *, tt, nt, rev):
    step = pl.program_id(0)
    slot_w = step % 2
    slot_r = 1 - slot_w
    n_halves = tt // (2 * CHUNK)

    @pl.when(step == 0)
    def _():
        pre_b_ref[1] = jnp.zeros(pre_b_ref.shape[1:], BF16)
        pre_f_ref[1] = jnp.zeros(pre_f_ref.shape[1:], F32)
        res_b_ref[0] = jnp.zeros(res_b_ref.shape[1:], BF16)
        res_f_ref[0] = jnp.zeros(res_f_ref.shape[1:], F32)
        state_ref[...] = jnp.zeros_like(state_ref)

    trow = lax.broadcasted_iota(jnp.int32, (CHUNK, CHUNK), 0)
    tcol = lax.broadcasted_iota(jnp.int32, (CHUNK, CHUNK), 1)
    tri = jnp.where((tcol >= trow) if rev else (tcol <= trow), 1.0, 0.0).astype(BF16)

    def prep_task(g):
        rows = slice(g * 2 * CHUNK, (g + 1) * 2 * CHUNK)
        lora = zs_ref[rows, 3 * HEADS_W:RWKV_IN]
        xdec = w0_ref[...] + _dot(jnp.tanh(lora).astype(BF16), wdec_ref[...])
        lw = (-math.exp(-0.5)) * _sigmoid(xdec)
        yield
        asig = _sigmoid(a0_ref[...] + _dot(lora.astype(BF16), wa_ref[...]))
        yield
        k = zs_ref[rows, HEADS_W:2 * HEADS_W]
        kk = k * kk_ref[...]
        ssq = _segsum(kk * kk, seg_ref[...])
        yield
        kk = kk / jnp.maximum(jnp.sqrt(ssq), 1e-12)
        kdir = k * (1.0 + (asig - 1.0) * ka_ref[...])
        bvec = kk * asig
        yield
        pack = lambda x: jnp.concatenate([x[:CHUNK], x[CHUNK:]], axis=1)
        lw_p = pack(lw)
        lw_hi, lw_lo = _split2(lw_p)
        cum = _dot(tri, lw_hi) + _dot(tri, lw_lo)
        yield
        last = 0 if rev else CHUNK - 1
        tot = cum[last:last + 1, :]
        e_neg = jnp.exp(-cum)
        e_end = jnp.exp(tot - cum)
        yield
        r_t = pack(zs_ref[rows, 0:HEADS_W]) * jnp.exp(cum)
        pre_f_ref[slot_w, g, 0:CHUNK, :] = r_t
        pre_f_ref[slot_w, g, CHUNK:CHUNK + 1, :] = jnp.exp(tot)
        yield
        kd_p, bv_p = pack(kdir), pack(bvec)
        out = dict(r_t=r_t, a_t=-pack(kk) * jnp.exp(cum - lw_p), b_t=bv_p * e_neg,
                   k_t=kd_p * e_neg, b_e=bv_p * e_end, k_e=kd_p * e_end,
                   v=pack(zs_ref[rows, 2 * HEADS_W:3 * HEADS_W]))
        for i, name in enumerate(_PRE_BF16):
            pre_b_ref[slot_w, g, i] = out[name].astype(BF16)
            yield

    def load_pre(g):
        pre = {name: pre_b_ref[slot_r, g, i] for i, name in enumerate(_PRE_BF16)}
        pre["r_f"] = pre_f_ref[slot_r, g, 0:CHUNK, :]
        pre["p_end"] = pre_f_ref[slot_r, g, CHUNK:CHUNK + 1, :]
        return pre

    gw = 2 * LANES
    head0 = lax.broadcasted_iota(jnp.int32, (1, LANES), 1) < HEAD_DIM
    tpos = lax.broadcasted_iota(jnp.int32, (CHUNK, gw), 0)
    spos = lax.broadcasted_iota(jnp.int32, (CHUNK, gw), 1) % CHUNK
    if rev:
        incl, strict = spos >= tpos, spos > tpos
    else:
        incl, strict = spos <= tpos, spos < tpos
    eye_g = jnp.where(spos == tpos, 1.0, 0.0)
    prow = lax.broadcasted_iota(jnp.int32, (LANES, LANES), 0)
    pcol = lax.broadcasted_iota(jnp.int32, (LANES, LANES), 1)
    pair_diag = (prow // HEAD_DIM) == (pcol // HEAD_DIM)

    def bd(xb):
        lo, hi = xb[:, :LANES], xb[:, LANES:]
        z = jnp.zeros_like(lo)
        return jnp.concatenate([
            jnp.concatenate([jnp.where(head0, lo, z), z], axis=1),
            jnp.concatenate([jnp.where(head0, z, lo), z], axis=1),
            jnp.concatenate([z, jnp.where(head0, hi, z)], axis=1),
            jnp.concatenate([z, jnp.where(head0, z, hi)], axis=1)], axis=0)

    def piece(x, p, cc):
        lane0 = cc * HEADS_W + p * LANES
        return x[:, lane0:lane0 + LANES]

    def grp(x, p):
        return jnp.concatenate([piece(x, p, 0), piece(x, p, 1)], axis=1)

    def sibling(m):
        return jnp.logical_and(tpos // (2 * m) == spos // (2 * m), tpos // m != spos // m)

    pairs = range(N_PAIRS)
    cc_order = (1, 0) if rev else (0, 1)

    def group_task(pre, p, g):
        rg = grp(pre["r_f"], p)
        ag_b = grp(pre["a_t"], p)
        lhs = jnp.concatenate([ag_b, grp(pre["r_t"], p)], axis=0)
        s1 = _dot_tb(lhs, bd(grp(pre["b_t"], p)))
        yield
        s2 = _dot_tb(lhs, bd(grp(pre["k_t"], p)))
        yield
        a_ab = jnp.where(strict, s1[:CHUNK], 0.0)
        a_rb = jnp.where(incl, s1[CHUNK:], 0.0).astype(BF16)
        a_ak = jnp.where(strict, s2[:CHUNK], 0.0).astype(BF16)
        a_rk = jnp.where(incl, s2[CHUNK:], 0.0).astype(BF16)
        tinv = eye_g + jnp.where(sibling(1), a_ab, 0.0)
        m = 2
        while m < CHUNK:
            a_off = jnp.where(sibling(m), a_ab, 0.0).astype(BF16)
            db = tinv.astype(BF16)
            yb = _dot(a_off, bd(db)).astype(BF16)
            yield
            tinv = tinv + _dot(db, bd(yb))
            yield
            m *= 2
        tb = tinv.astype(BF16)
        vbd = bd(grp(pre["v"], p))
        w2 = _dot(a_ak, vbd).astype(BF16)
        yield
        ta = _dot(tb, bd(ag_b))
        yield
        tv = _dot(tb, bd(w2))
        yield
        qg = rg + _dot(a_rb, bd(ta.astype(BF16)))
        yield
        yg = _dot(a_rb, bd(tv.astype(BF16))) + _dot(a_rk, vbd)
        yield
        for cc in cc_order:
            half = slice(cc * LANES, (cc + 1) * LANES)
            be = piece(pre["b_e"], p, cc)
            xt = jnp.where(pair_diag, _dot_ta(ta[:, half].astype(BF16), be), 0.0)
            yield
            ht = jnp.where(pair_diag, _dot_ta(
                jnp.concatenate([tv[:, half].astype(BF16), piece(pre["v"], p, cc)], axis=0),
                jnp.concatenate([be, piece(pre["k_e"], p, cc)], axis=0)), 0.0)
            yield
            i = p * (2 * n_halves) + 2 * g + cc
            res_b_ref[slot_r, i, 0:LANES, :] = xt.astype(BF16)
            res_b_ref[slot_r, i, LANES:LANES + CHUNK, :] = qg[:, half].astype(BF16)
            res_f_ref[slot_r, i, 0:LANES, :] = ht
            res_f_ref[slot_r, i, LANES:LANES + CHUNK, :] = yg[:, half]
            res_f_ref[slot_r, i, LANES + CHUNK:LANES + CHUNK + 1, :] = piece(pre["p_end"], p, cc)

    fresh = (step - 2) % nt == 0
    state = {p: jnp.where(fresh, 0.0, state_ref[p]) for p in pairs}

    def chain_task():
        for g in halves:
            for cc in cc_order:
                c = 2 * g + cc
                sb = {p: state[p].astype(BF16) for p in pairs}
                for p in pairs:
                    i = p * (2 * n_halves) + c
                    y_ref[c * CHUNK:(c + 1) * CHUNK, p * LANES:(p + 1) * LANES] = (
                        res_f_ref[slot_w, i, LANES:LANES + CHUNK, :]
                        + _dot_tb(res_b_ref[slot_w, i, LANES:LANES + CHUNK, :], sb[p]))
                for p in pairs:
                    i = p * (2 * n_halves) + c
                    state[p] = (state[p] * res_f_ref[slot_w, i, LANES + CHUNK:LANES + CHUNK + 1, :]
                                + _dot(sb[p], res_b_ref[slot_w, i, 0:LANES, :])
                                + res_f_ref[slot_w, i, 0:LANES, :])
                yield

    halves = list(range(n_halves))
    if rev:
        halves.reverse()
    tasks = [group_task(load_pre(g), p, g) for g in halves for p in pairs]
    preps = [prep_task(g) for g in halves]
    chain = chain_task()
    rounds = 0
    while tasks:
        tasks = [task for task in tasks if next(task, _DONE) is not _DONE]
        for _ in range(SCAN_PREP_PER_STAGE):
            if preps and next(preps[0], _DONE) is _DONE:
                preps.pop(0)
        if rounds % SCAN_CHAIN_EVERY == SCAN_CHAIN_EVERY - 1:
            next(chain, _DONE)
        rounds += 1
    for task in preps + [chain]:
        for _ in task:
            pass
    for p in pairs:
        state_ref[p] = state[p]


def _scan(zs3, w0, wdec, a0, wa, k_k, k_a, seg, b, t, tt, rev):
    nt = t // tt
    total = b * nt
    zs_tiles = zs3.reshape(total, tt, RWKV_IN)

    def tile_of(pos):
        if not rev:
            return pos
        return (pos // nt) * nt + (nt - 1 - pos % nt)

    const = lambda s: (0, 0)
    vec = lambda w: pl.BlockSpec((1, w), const)
    in_specs = [pl.BlockSpec((None, tt, RWKV_IN),
                             lambda s: (tile_of(jnp.minimum(s, total - 1)), 0, 0)),
                vec(HEADS_W), pl.BlockSpec((LORA_W, HEADS_W), const),
                vec(HEADS_W), pl.BlockSpec((LORA_W, HEADS_W), const),
                vec(HEADS_W), vec(HEADS_W), pl.BlockSpec((LANES, LANES), const)]
    n_halves = tt // (2 * CHUNK)
    packed_w = 2 * HEADS_W
    n_pc = N_PAIRS * 2 * n_halves
    y = pl.pallas_call(
        functools.partial(_scan_kernel, tt=tt, nt=nt, rev=rev),
        grid=(total + 2,),
        in_specs=in_specs,
        out_specs=pl.BlockSpec((None, tt, HEADS_W),
                               lambda s: (tile_of(jnp.maximum(s - 2, 0)), 0, 0)),
        out_shape=jax.ShapeDtypeStruct((total, tt, HEADS_W), F32),
        scratch_shapes=[pltpu.VMEM((N_PAIRS, LANES, LANES), F32),
                        pltpu.VMEM((2, n_halves, len(_PRE_BF16), CHUNK, packed_w), BF16),
                        pltpu.VMEM((2, n_halves, CHUNK + HALO_ROWS, packed_w), F32),
                        pltpu.VMEM((2, n_pc, LANES + CHUNK, LANES), BF16),
                        pltpu.VMEM((2, n_pc, LANES + CHUNK + HALO_ROWS, LANES), F32)],
        compiler_params=pltpu.CompilerParams(
            dimension_semantics=("arbitrary",), vmem_limit_bytes=VMEM_LIMIT),
        name="rwkv_scan_bwd" if rev else "rwkv_scan_fwd",
    )(zs_tiles, w0, wdec, a0, wa, k_k, k_a, seg)
    return y.reshape(b * t, HEADS_W)


def _mix_ffn_kernel(x_ref, att_ref, conv_ref, zs_ref, yf_ref, yb_ref,
                    a0_ref, wa_ref, wg_ref, ka_ref, rk_ref, lnw_ref, lnb_ref, seg_ref,
                    ga_ref, gb_ref, wout_ref, gmix_ref, gpre_ref, w1_ref, w2_ref, gffn_ref,
                    o_ref, *, ff_tile):
    seg = seg_ref[...]

    r = zs_ref[:, 0:HEADS_W]
    k = zs_ref[:, HEADS_W:2 * HEADS_W]
    v = zs_ref[:, 2 * HEADS_W:3 * HEADS_W]
    lora = zs_ref[:, 3 * HEADS_W:RWKV_IN]
    lora_b = lora.astype(BF16)
    asum = (_sigmoid(a0_ref[0:1, :] + _dot(lora_b, wa_ref[0]))
            + _sigmoid(a0_ref[1:2, :] + _dot(lora_b, wa_ref[1])))
    gate = _dot(_sigmoid(lora).astype(BF16), wg_ref[...])
    ksum = k * (2.0 + (asum - 2.0) * ka_ref[...])
    bonus = _segsum(r * ksum * rk_ref[...], seg) * v
    y = yf_ref[...] + yb_ref[...]
    yc = y - _segsum(y, seg) * (1.0 / HEAD_DIM)
    var = _segsum(yc * yc, seg) * (1.0 / HEAD_DIM)
    yn = yc * lax.rsqrt(var + LNX_EPS) * lnw_ref[...] + lnb_ref[...]
    y_rwkv = ((yn + bonus) * gate).astype(BF16)

    ya = _rms(att_ref[...], ga_ref[...]).astype(BF16)
    yb = _rms(conv_ref[...], gb_ref[...]).astype(BF16)
    mix = (_dot(ya, wout_ref[0:HEADS_W, :])
           + _dot(yb, wout_ref[HEADS_W:HEADS_W + CONV_W, :])
           + _dot(y_rwkv, wout_ref[HEADS_W + CONV_W:, :]))
    x = x_ref[...] + _rms(mix, gmix_ref[...])

    h = _rms(x, gpre_ref[...]).astype(BF16)
    acc = None
    for j in range(D_FF // ff_tile):
        sl = slice(j * ff_tile, (j + 1) * ff_tile)
        a = jnp.maximum(_dot(h, w1_ref[:, sl]), 0.0)
        part = _dot((a * a).astype(BF16), w2_ref[sl, :])
        acc = part if acc is None else acc + part
    o_ref[...] = x + _rms(acc, gffn_ref[...])


def _mix_ffn(x2, att, conv, zs, yf, yb, lp, tm):
    n = x2.shape[0]
    row = lambda i: (i, 0)
    tile = lambda w: pl.BlockSpec((tm, w), row)
    fixed = lambda *shape: pl.BlockSpec(shape, lambda i: (0,) * len(shape),
                                        pipeline_mode=pl.Buffered(1))
    return pl.pallas_call(
        functools.partial(_mix_ffn_kernel, ff_tile=1024),
        grid=(n // tm,),
        in_specs=[tile(D_MODEL), tile(HEADS_W), tile(CONV_W), tile(RWKV_IN), tile(HEADS_W),
                  tile(HEADS_W),
                  fixed(2, HEADS_W), fixed(2, LORA_W, HEADS_W), fixed(LORA_W, HEADS_W),
                  fixed(1, HEADS_W), fixed(1, HEADS_W), fixed(1, HEADS_W), fixed(1, HEADS_W),
                  fixed(LANES, LANES),
                  fixed(1, HEADS_W), fixed(1, CONV_W), fixed(D_MODEL, D_MODEL),
                  fixed(1, D_MODEL), fixed(1, D_MODEL), fixed(D_MODEL, D_FF),
                  fixed(D_FF, D_MODEL), fixed(1, D_MODEL)],
        out_specs=tile(D_MODEL),
        out_shape=jax.ShapeDtypeStruct((n, D_MODEL), F32),
        compiler_params=pltpu.CompilerParams(
            dimension_semantics=("parallel",), vmem_limit_bytes=VMEM_LIMIT),
        name="mix_ffn",
    )(x2, att, conv, zs, yf, yb,
      lp["a0"], lp["wa"], lp["wg"], lp["k_a"], lp["r_k"], lp["lnx_w"], lp["lnx_b"], lp["seg"],
      lp["attn_g"], lp["conv_g"], lp["w_out"], lp["g_mix_post"], lp["g_ffn_pre"],
      lp["w1"], lp["w2"], lp["g_ffn_post"])


def _pad_rows(w, start):
    return jnp.zeros((LORA_W, HEADS_W), F32).at[start:start + w.shape[0]].set(w)


def _layer_params(l, p):
    row = lambda a: a.reshape(1, -1).astype(F32)
    seg = np.kron(np.eye(LANES // HEAD_DIM, dtype=np.float32),
                  np.ones((HEAD_DIM, HEAD_DIM), np.float32))
    return dict(
        g_mix_pre=row(p["norm_mix_pre"][l]), g_mix_post=row(p["norm_mix_post"][l]),
        g_ffn_pre=row(p["norm_ffn_pre"][l]), g_ffn_post=row(p["norm_ffn_post"][l]),
        w_in=p["w_in"][l].astype(BF16), w_out=p["w_out"][l].astype(BF16),
        attn_g=row(p["attn_out_g"][l]), conv_w=p["conv_w"][l].astype(F32),
        conv_g=row(p["conv_out_g"][l]), mu=row(p["rwkv_mu"][l]),
        w0=p["decay_w0"][l].astype(F32), a0=p["iclr_a0"][l].astype(F32),
        wdec=jnp.stack([_pad_rows(p["decay_up"][l, d], 0) for d in range(2)]).astype(BF16),
        wa=jnp.stack([_pad_rows(p["iclr_up"][l, d], DECAY_RANK) for d in range(2)]).astype(BF16),
        wg=_pad_rows(p["gate_up"][l], DECAY_RANK + ICLR_RANK).astype(BF16),
        k_k=row(p["k_k"][l]), k_a=row(p["k_a"][l]), r_k=row(p["r_k"][l]),
        lnx_w=row(p["lnx_w"][l]), lnx_b=row(p["lnx_b"][l]),
        w1=p["ffn_w1"][l].astype(BF16), w2=p["ffn_w2"][l].astype(BF16),
        seg=jnp.asarray(seg, BF16),
    )


def _layer(x2, lp, biases, b, t):
    n = b * t
    tm = min(512, n)
    tt = min(SCAN_TILE, t)
    zq, conv, zs = _inproj(x2, lp["g_mix_pre"], lp["w_in"], lp["mu"], lp["conv_w"], tm, t)
    att = _attention(zq, biases, b, t)
    zs3 = zs.reshape(b, t, RWKV_IN)
    ys = [_scan(zs3, lp["w0"][d:d + 1], lp["wdec"][d], lp["a0"][d:d + 1], lp["wa"][d],
                lp["k_k"], lp["k_a"], lp["seg"], b, t, tt, rev=bool(d)).reshape(n, HEADS_W)
          for d in range(2)]
    return _mix_ffn(x2, att, conv, zs, ys[0], ys[1], lp, tm)


def _trunk(x, rel_bias, layers):
    b, t, _ = x.shape
    biases = [_attn_bias(rel_bias, t, dil) for dil in DILATIONS]
    x2 = x.reshape(b * t, D_MODEL)
    for lp in layers:
        x2 = _layer(x2, lp, biases, b, t)
    return x2.reshape(b, t, D_MODEL)


def kernel(x_prompt, x_sample, rel_bias, norm_mix_pre, norm_mix_post, norm_ffn_pre, norm_ffn_post, w_in, w_out, attn_out_g, conv_w, conv_out_g, rwkv_mu, decay_w0, decay_up, iclr_a0, iclr_up, gate_up, k_k, k_a, r_k, lnx_w, lnx_b, ffn_w1, ffn_w2):
    p = dict(norm_mix_pre=norm_mix_pre, norm_mix_post=norm_mix_post, norm_ffn_pre=norm_ffn_pre,
             norm_ffn_post=norm_ffn_post, w_in=w_in, w_out=w_out, attn_out_g=attn_out_g,
             conv_w=conv_w, conv_out_g=conv_out_g, rwkv_mu=rwkv_mu, decay_w0=decay_w0,
             decay_up=decay_up, iclr_a0=iclr_a0, iclr_up=iclr_up, gate_up=gate_up, k_k=k_k,
             k_a=k_a, r_k=r_k, lnx_w=lnx_w, lnx_b=lnx_b, ffn_w1=ffn_w1, ffn_w2=ffn_w2)
    layers = [_layer_params(l, p) for l in range(w_in.shape[0])]
    return (_trunk(x_prompt, rel_bias, layers), _trunk(x_sample, rel_bias, layers))
```

```python
import functools
import math

import numpy as np
import jax
import jax.numpy as jnp
from jax import lax
from jax.experimental import pallas as pl
from jax.experimental.pallas import tpu as pltpu

F32 = jnp.float32
BF16 = jnp.bfloat16

D_MODEL = 1024
HEAD_DIM = 64
N_HEADS = 6
HEADS_W = N_HEADS * HEAD_DIM
CONV_W = 256
QKV_W = 3 * HEADS_W
CONVIN_W = 3 * CONV_W
LORA_W = 128
DECAY_RANK = 32
ICLR_RANK = 32
RWKV_IN = 3 * HEADS_W + LORA_W
IN_WIDTH = QKV_W + CONVIN_W + RWKV_IN
D_FF = 4 * D_MODEL
DILATIONS = (1, 4, 16)
KEYS_PER_SIDE = 64
ATTN_UNROLL = 8
SCAN_TILE = 512
SCAN_CHAIN_EVERY = 2
SCAN_PREP_PER_STAGE = 1
_DONE = object()
N_BUCKETS = 32
BUCKET_MAX_DIST = 1024
LOG2E = math.log2(math.e)
RMS_EPS = 1e-6
LNX_EPS = 64e-5
CHUNK = 64
LANES = 128
N_PAIRS = HEADS_W // LANES
HALO_ROWS = 8
VMEM_LIMIT = 56 * 1024 * 1024


def _rms(x, g, eps=RMS_EPS):
    return x * lax.rsqrt(jnp.mean(x * x, axis=-1, keepdims=True) + eps) * g


def _sigmoid(x):
    return 1.0 / (1.0 + jnp.exp(-x))


def _dot(a, b):
    return jnp.dot(a, b, preferred_element_type=F32)


def _dot_tb(a, b):
    return lax.dot_general(a, b, (((1,), (1,)), ((), ())), preferred_element_type=F32)


def _dot_ta(a, b):
    return lax.dot_general(a, b, (((0,), (0,)), ((), ())), preferred_element_type=F32)


def _split2(x):
    hi = x.astype(BF16)
    lo = (x - hi.astype(F32)).astype(BF16)
    return hi, lo


def _segsum(x, pair_ones):
    xb = x.astype(BF16)
    return jnp.concatenate([_dot(xb[:, p * LANES:(p + 1) * LANES], pair_ones)
                            for p in range(x.shape[1] // LANES)], axis=1)


def _neighbours(main, prev_row, next_row):
    tt = main.shape[0]
    row = lax.broadcasted_iota(jnp.int32, main.shape, 0)
    prev = jnp.where(row == 0, prev_row, pltpu.roll(main, 1, 0))
    nxt = jnp.where(row == tt - 1, next_row, pltpu.roll(main, tt - 1, 0))
    return prev, nxt


def _inproj_kernel(x_ref, xp_ref, xn_ref, g_ref, w_ref, mu_ref, cw_ref,
                   zq_ref, conv_ref, zs_ref, *, tm, t):
    i = pl.program_id(0)
    x_ext = jnp.concatenate([x_ref[...], xp_ref[...], xn_ref[...]], axis=0)
    h_ext = _rms(x_ext, g_ref[...]).astype(BF16)
    zq_ref[...] = _dot(h_ext[:tm], w_ref[:, 0:QKV_W])
    z_ext = _dot(h_ext, w_ref[:, QKV_W:IN_WIDTH])
    prev_row = jnp.where((i * tm) % t == 0, 0.0,
                         z_ext[tm + HALO_ROWS - 1:tm + HALO_ROWS, :])
    next_row = jnp.where(((i + 1) * tm) % t == 0, 0.0,
                         z_ext[tm + HALO_ROWS:tm + HALO_ROWS + 1, :])
    z = z_ext[:tm]

    gated = lambda a: a[:, CONV_W:2 * CONV_W] * a[:, 2 * CONV_W:3 * CONV_W]
    u = gated(z)
    u_prev, u_next = _neighbours(u, gated(prev_row), gated(next_row))
    conv_ref[...] = z[:, 0:CONV_W] * (
        u_prev * cw_ref[0:1, :] + u * cw_ref[1:2, :] + u_next * cw_ref[2:3, :])

    zc = z[:, CONVIN_W:]
    c_prev, c_next = _neighbours(zc, prev_row[:, CONVIN_W:], next_row[:, CONVIN_W:])
    zs_ref[...] = zc + (0.5 * (c_prev + c_next) - zc) * mu_ref[...]


def _inproj(x2, g, w_bf16, mu, cw, tm, t):
    n = x2.shape[0]
    per = tm // HALO_ROWS
    last = n // HALO_ROWS - 1
    row = lambda i: (i, 0)
    const = lambda i: (0, 0)
    return pl.pallas_call(
        functools.partial(_inproj_kernel, tm=tm, t=t),
        grid=(n // tm,),
        in_specs=[pl.BlockSpec((tm, D_MODEL), row),
                  pl.BlockSpec((HALO_ROWS, D_MODEL), lambda i: (jnp.maximum(i * per - 1, 0), 0)),
                  pl.BlockSpec((HALO_ROWS, D_MODEL), lambda i: (jnp.minimum((i + 1) * per, last), 0)),
                  pl.BlockSpec((1, D_MODEL), const),
                  pl.BlockSpec((D_MODEL, IN_WIDTH), const),
                  pl.BlockSpec((1, RWKV_IN), const),
                  pl.BlockSpec((3, CONV_W), const)],
        out_specs=[pl.BlockSpec((tm, QKV_W), row),
                   pl.BlockSpec((tm, CONV_W), row),
                   pl.BlockSpec((tm, RWKV_IN), row)],
        out_shape=[jax.ShapeDtypeStruct((n, QKV_W), F32),
                   jax.ShapeDtypeStruct((n, CONV_W), F32),
                   jax.ShapeDtypeStruct((n, RWKV_IN), F32)],
        compiler_params=pltpu.CompilerParams(
            dimension_semantics=("parallel",), vmem_limit_bytes=VMEM_LIMIT),
        name="inproj",
    )(x2, x2, x2, g, w_bf16, mu, cw)


def _t5_bucket(rel):
    half = N_BUCKETS // 2
    max_exact = half // 2
    ret = np.where(rel > 0, half, 0)
    n = np.abs(rel)
    large = max_exact + (np.log(np.maximum(n, 1) / max_exact)
                         / np.log(BUCKET_MAX_DIST / max_exact) * (half - max_exact)).astype(np.int32)
    large = np.minimum(large, half - 1)
    return (ret + np.where(n < max_exact, n, large)).astype(np.int32)


def _attn_cfg(t, dil):
    cls_len = t // dil
    bq = min(128, cls_len)
    wk = min(bq + 2 * KEYS_PER_SIDE, cls_len)
    return cls_len, bq, wk, cls_len // bq


def _attn_bias(rel_bias, t, dil):
    _, bq, wk, _ = _attn_cfg(t, dil)
    out = []
    for shift in (0, KEYS_PER_SIDE, wk - bq):
        rel = np.arange(wk)[None, :] - np.arange(bq)[:, None] - shift
        valid = np.abs(rel) <= KEYS_PER_SIDE
        onehot = jnp.asarray(_t5_bucket(rel * dil))[:, :, None] == jnp.arange(N_BUCKETS)
        table = jnp.transpose(rel_bias.astype(F32))[:, None, None, :]
        bias = jnp.sum(jnp.where(onehot[None], table, 0.0), axis=-1)
        out.append(jnp.where(valid[None], bias * LOG2E, -jnp.inf))
    return jnp.stack(out)


def _attn_kernel(q_ref, k_ref, v_ref, b0_ref, b1_ref, b2_ref, o_ref, m_ref, s_ref, *, t):
    lane = lax.broadcasted_iota(jnp.int32, (1, LANES), 1)
    head0 = lane < HEAD_DIM
    bias_refs = (b0_ref, b1_ref, b2_ref)

    for di, dil in reversed(list(enumerate(DILATIONS))):
        cls_len, bq, wk, nb = _attn_cfg(t, dil)
        bias_ref = bias_refs[di]
        first_pass = dil == DILATIONS[-1]

        def rows(start, size, dil=dil):
            if dil == 1:
                return pl.ds(start, size)
            return pl.ds(start, size, stride=dil)

        def do_blocks(blocks, first_pass=first_pass, dil=dil, bq=bq, wk=wk, cls_len=cls_len,
                      bias_ref=bias_ref, rows=rows):
            qrows, q, kw, vw = [], [], [], []
            for c, blk, placement in blocks:
                m0 = blk * bq
                ws = (0, m0 - KEYS_PER_SIDE, cls_len - wk)[placement]
                qrows.append(rows(c + dil * m0, bq))
                krows = rows(c + dil * ws, wk)
                q.append(q_ref[qrows[-1], :] * (HEAD_DIM ** -0.5 * LOG2E))
                kw.append(k_ref[krows, :].astype(BF16))
                vw.append(v_ref[krows, :].astype(BF16))
            items = [(i, h) for i in range(len(blocks)) for h in range(2)]
            hmask = (head0, jnp.logical_not(head0))
            logits = {(i, h): _dot_tb(jnp.where(hmask[h], q[i], 0.0).astype(BF16), kw[i])
                      + bias_ref[blocks[i][2], h] for i, h in items}
            mh = {k: jnp.max(logits[k], axis=-1, keepdims=True) for k in items}
            p = {k: jnp.exp2(logits[k] - mh[k]) for k in items}
            sh = {k: jnp.sum(p[k], axis=-1, keepdims=True) for k in items}
            oh = {(i, h): _dot(p[(i, h)].astype(BF16), vw[i]) for i, h in items}
            for i in range(len(blocks)):
                m_blk = jnp.where(head0, mh[(i, 0)], mh[(i, 1)])
                s_blk = jnp.where(head0, sh[(i, 0)], sh[(i, 1)])
                o_blk = jnp.where(head0, oh[(i, 0)], oh[(i, 1)])
                if first_pass:
                    m_ref[qrows[i], :] = m_blk
                    s_ref[qrows[i], :] = s_blk
                    o_ref[qrows[i], :] = o_blk
                else:
                    m_old = m_ref[qrows[i], :]
                    m_new = jnp.maximum(m_old, m_blk)
                    a_old = jnp.exp2(m_old - m_new)
                    a_blk = jnp.exp2(m_blk - m_new)
                    m_ref[qrows[i], :] = m_new
                    s_ref[qrows[i], :] = s_ref[qrows[i], :] * a_old + s_blk * a_blk
                    o_ref[qrows[i], :] = o_ref[qrows[i], :] * a_old + o_blk * a_blk

        def loop(n, body):
            if n == 1:
                body(0)
            else:
                lax.fori_loop(0, n, lambda i, carry: (body(i), carry)[1], 0)

        edge = [(0, 0)] + ([(nb - 1, 2)] if nb > 1 else [])
        cb = math.gcd(dil, max(1, ATTN_UNROLL // len(edge)))
        loop(dil // cb, lambda i, edge=edge, cb=cb, do_blocks=do_blocks: do_blocks(
            [(i * cb + j, blk, placement) for j in range(cb) for blk, placement in edge]))
        n_mid = max(nb - 2, 0)
        if n_mid:
            u = max(d for d in range(1, ATTN_UNROLL + 1) if n_mid % d == 0)
            per = n_mid // u
            loop(dil * per, lambda i, u=u, per=per, do_blocks=do_blocks: do_blocks(
                [(i // per, 1 + (i % per) * u + j, 1) for j in range(u)]))

    rb = min(256, t)

    def norm_body(i, carry):
        sl = pl.ds(pl.multiple_of(i * rb, rb), rb)
        o_ref[sl, :] = o_ref[sl, :] / s_ref[sl, :]
        return carry

    lax.fori_loop(0, t // rb, norm_body, 0)


def _attention(zq, biases, b, t):
    zq3 = zq.reshape(b, t, QKV_W)
    in_specs = [pl.BlockSpec((None, t, LANES), lambda i, hp: (i, 0, hp)),
                pl.BlockSpec((None, t, LANES), lambda i, hp: (i, 0, N_PAIRS + hp)),
                pl.BlockSpec((None, t, LANES), lambda i, hp: (i, 0, 2 * N_PAIRS + hp))]
    for bias in biases:
        in_specs.append(pl.BlockSpec((3, 2) + bias.shape[2:], lambda i, hp: (0, hp, 0, 0)))
    out = pl.pallas_call(
        functools.partial(_attn_kernel, t=t),
        grid=(b, N_PAIRS),
        in_specs=in_specs,
        out_specs=pl.BlockSpec((None, t, LANES), lambda i, hp: (i, 0, hp)),
        out_shape=jax.ShapeDtypeStruct((b, t, HEADS_W), F32),
        scratch_shapes=[pltpu.VMEM((t, LANES), F32), pltpu.VMEM((t, LANES), F32)],
        compiler_params=pltpu.CompilerParams(
            dimension_semantics=("parallel", "parallel"), vmem_limit_bytes=VMEM_LIMIT),
        name="dilated_attention",
    )(zq3, zq3, zq3, *biases)
    return out.reshape(b * t, HEADS_W)


_PRE_BF16 = ("r_t", "a_t", "b_t", "k_t", "b_e", "k_e", "v")


def _scan_kernel(zs_ref, w0_ref, wdec_ref, a0_ref, wa_ref, kk_ref, ka_ref, seg_ref,
                 y_ref, state_ref, pre_b_ref, pre_f_ref, res_b_ref, res_f_ref,
                 *, tt, nt, rev):
    step = pl.program_id(0)
    slot_w = step % 2
    slot_r = 1 - slot_w
    n_halves = tt // (2 * CHUNK)

    @pl.when(step == 0)
    def _():
        pre_b_ref[1] = jnp.zeros(pre_b_ref.shape[1:], BF16)
        pre_f_ref[1] = jnp.zeros(pre_f_ref.shape[1:], F32)
        res_b_ref[0] = jnp.zeros(res_b_ref.shape[1:], BF16)
        res_f_ref[0] = jnp.zeros(res_f_ref.shape[1:], F32)
        state_ref[...] = jnp.zeros_like(state_ref)

    trow = lax.broadcasted_iota(jnp.int32, (CHUNK, CHUNK), 0)
    tcol = lax.broadcasted_iota(jnp.int32, (CHUNK, CHUNK), 1)
    tri = jnp.where((tcol >= trow) if rev else (tcol <= trow), 1.0, 0.0).astype(BF16)

    def prep_task(g):
        rows = slice(g * 2 * CHUNK, (g + 1) * 2 * CHUNK)
        lora = zs_ref[rows, 3 * HEADS_W:RWKV_IN]
        xdec = w0_ref[...] + _dot(jnp.tanh(lora).astype(BF16), wdec_ref[...])
        lw = (-math.exp(-0.5)) * _sigmoid(xdec)
        yield
        asig = _sigmoid(a0_ref[...] + _dot(lora.astype(BF16), wa_ref[...]))
        yield
        k = zs_ref[rows, HEADS_W:2 * HEADS_W]
        kk = k * kk_ref[...]
        ssq = _segsum(kk * kk, seg_ref[...])
        yield
        kk = kk / jnp.maximum(jnp.sqrt(ssq), 1e-12)
        kdir = k * (1.0 + (asig - 1.0) * ka_ref[...])
        bvec = kk * asig
        yield
        pack = lambda x: jnp.concatenate([x[:CHUNK], x[CHUNK:]], axis=1)
        lw_p = pack(lw)
        lw_hi, lw_lo = _split2(lw_p)
        cum = _dot(tri, lw_hi) + _dot(tri, lw_lo)
        yield
        last = 0 if rev else CHUNK - 1
        tot = cum[last:last + 1, :]
        e_neg = jnp.exp(-cum)
        e_end = jnp.exp(tot - cum)
        yield
        r_t = pack(zs_ref[rows, 0:HEADS_W]) * jnp.exp(cum)
        pre_f_ref[slot_w, g, 0:CHUNK, :] = r_t
        pre_f_ref[slot_w, g, CHUNK:CHUNK + 1, :] = jnp.exp(tot)
        yield
        kd_p, bv_p = pack(kdir), pack(bvec)
        out = dict(r_t=r_t, a_t=-pack(kk) * jnp.exp(cum - lw_p), b_t=bv_p * e_neg,
                   k_t=kd_p * e_neg, b_e=bv_p * e_end, k_e=kd_p * e_end,
                   v=pack(zs_ref[rows, 2 * HEADS_W:3 * HEADS_W]))
        for i, name in enumerate(_PRE_BF16):
            pre_b_ref[slot_w, g, i] = out[name].astype(BF16)
            yield

    def load_pre(g):
        pre = {name: pre_b_ref[slot_r, g, i] for i, name in enumerate(_PRE_BF16)}
        pre["r_f"] = pre_f_ref[slot_r, g, 0:CHUNK, :]
        pre["p_end"] = pre_f_ref[slot_r, g, CHUNK:CHUNK + 1, :]
        return pre

    gw = 2 * LANES
    head0 = lax.broadcasted_iota(jnp.int32, (1, LANES), 1) < HEAD_DIM
    tpos = lax.broadcasted_iota(jnp.int32, (CHUNK, gw), 0)
    spos = lax.broadcasted_iota(jnp.int32, (CHUNK, gw), 1) % CHUNK
    if rev:
        incl, strict = spos >= tpos, spos > tpos
    else:
        incl, strict = spos <= tpos, spos < tpos
    eye_g = jnp.where(spos == tpos, 1.0, 0.0)
    prow = lax.broadcasted_iota(jnp.int32, (LANES, LANES), 0)
    pcol = lax.broadcasted_iota(jnp.int32, (LANES, LANES), 1)
    pair_diag = (prow // HEAD_DIM) == (pcol // HEAD_DIM)

    def bd(xb):
        lo, hi = xb[:, :LANES], xb[:, LANES:]
        z = jnp.zeros_like(lo)
        return jnp.concatenate([
            jnp.concatenate([jnp.where(head0, lo, z), z], axis=1),
            jnp.concatenate([jnp.where(head0, z, lo), z], axis=1),
            jnp.concatenate([z, jnp.where(head0, hi, z)], axis=1),
            jnp.concatenate([z, jnp.where(head0, z, hi)], axis=1)], axis=0)

    def piece(x, p, cc):
        lane0 = cc * HEADS_W + p * LANES
        return x[:, lane0:lane0 + LANES]

    def grp(x, p):
        return jnp.concatenate([piece(x, p, 0), piece(x, p, 1)], axis=1)

    def sibling(m):
        return jnp.logical_and(tpos // (2 * m) == spos // (2 * m), tpos // m != spos // m)

    pairs = range(N_PAIRS)
    cc_order = (1, 0) if rev else (0, 1)

    def group_task(pre, p, g):
        rg = grp(pre["r_f"], p)
        ag_b = grp(pre["a_t"], p)
        lhs = jnp.concatenate([ag_b, grp(pre["r_t"], p)], axis=0)
        s1 = _dot_tb(lhs, bd(grp(pre["b_t"], p)))
        yield
        s2 = _dot_tb(lhs, bd(grp(pre["k_t"], p)))
        yield
        a_ab = jnp.where(strict, s1[:CHUNK], 0.0)
        a_rb = jnp.where(incl, s1[CHUNK:], 0.0).astype(BF16)
        a_ak = jnp.where(strict, s2[:CHUNK], 0.0).astype(BF16)
        a_rk = jnp.where(incl, s2[CHUNK:], 0.0).astype(BF16)
        tinv = eye_g + jnp.where(sibling(1), a_ab, 0.0)
        m = 2
        while m < CHUNK:
            a_off = jnp.where(sibling(m), a_ab, 0.0).astype(BF16)
            db = tinv.astype(BF16)
            yb = _dot(a_off, bd(db)).astype(BF16)
            yield
            tinv = tinv + _dot(db, bd(yb))
            yield
            m *= 2
        tb = tinv.astype(BF16)
        vbd = bd(grp(pre["v"], p))
        w2 = _dot(a_ak, vbd).astype(BF16)
        yield
        ta = _dot(tb, bd(ag_b))
        yield
        tv = _dot(tb, bd(w2))
        yield
        qg = rg + _dot(a_rb, bd(ta.astype(BF16)))
        yield
        yg = _dot(a_rb, bd(tv.astype(BF16))) + _dot(a_rk, vbd)
        yield
        for cc in cc_order:
            half = slice(cc * LANES, (cc + 1) * LANES)
            be = piece(pre["b_e"], p, cc)
            xt = jnp.where(pair_diag, _dot_ta(ta[:, half].astype(BF16), be), 0.0)
            yield
            ht = jnp.where(pair_diag, _dot_ta(
                jnp.concatenate([tv[:, half].astype(BF16), piece(pre["v"], p, cc)], axis=0),
                jnp.concatenate([be, piece(pre["k_e"], p, cc)], axis=0)), 0.0)
            yield
            i = p * (2 * n_halves) + 2 * g + cc
            res_b_ref[slot_r, i, 0:LANES, :] = xt.astype(BF16)
            res_b_ref[slot_r, i, LANES:LANES + CHUNK, :] = qg[:, half].astype(BF16)
            res_f_ref[slot_r, i, 0:LANES, :] = ht
            res_f_ref[slot_r, i, LANES:LANES + CHUNK, :] = yg[:, half]
            res_f_ref[slot_r, i, LANES + CHUNK:LANES + CHUNK + 1, :] = piece(pre["p_end"], p, cc)

    fresh = (step - 2) % nt == 0
    state = {p: jnp.where(fresh, 0.0, state_ref[p]) for p in pairs}

    def chain_task():
        for g in halves:
            for cc in cc_order:
                c = 2 * g + cc
                sb = {p: state[p].astype(BF16) for p in pairs}
                for p in pairs:
                    i = p * (2 * n_halves) + c
                    y_ref[c * CHUNK:(c + 1) * CHUNK, p * LANES:(p + 1) * LANES] = (
                        res_f_ref[slot_w, i, LANES:LANES + CHUNK, :]
                        + _dot_tb(res_b_ref[slot_w, i, LANES:LANES + CHUNK, :], sb[p]))
                for p in pairs:
                    i = p * (2 * n_halves) + c
                    state[p] = (state[p] * res_f_ref[slot_w, i, LANES + CHUNK:LANES + CHUNK + 1, :]
                                + _dot(sb[p], res_b_ref[slot_w, i, 0:LANES, :])
                                + res_f_ref[slot_w, i, 0:LANES, :])
                yield

    halves = list(range(n_halves))
    if rev:
        halves.reverse()
    tasks = [group_task(load_pre(g), p, g) for g in halves for p in pairs]
    preps = [prep_task(g) for g in halves]
    chain = chain_task()
    rounds = 0
    while tasks:
        tasks = [task for task in tasks if next(task, _DONE) is not _DONE]
        for _ in range(SCAN_PREP_PER_STAGE):
            if preps and next(preps[0], _DONE) is _DONE:
                preps.pop(0)
        if rounds % SCAN_CHAIN_EVERY == SCAN_CHAIN_EVERY - 1:
            next(chain, _DONE)
        rounds += 1
    for task in preps + [chain]:
        for _ in task:
            pass
    for p in pairs:
        state_ref[p] = state[p]


def _scan(zs3, w0, wdec, a0, wa, k_k, k_a, seg, b, t, tt, rev):
    nt = t // tt
    total = b * nt
    zs_tiles = zs3.reshape(total, tt, RWKV_IN)

    def tile_of(pos):
        if not rev:
            return pos
        return (pos // nt) * nt + (nt - 1 - pos % nt)

    const = lambda s: (0, 0)
    vec = lambda w: pl.BlockSpec((1, w), const)
    in_specs = [pl.BlockSpec((None, tt, RWKV_IN),
                             lambda s: (tile_of(jnp.minimum(s, total - 1)), 0, 0)),
                vec(HEADS_W), pl.BlockSpec((LORA_W, HEADS_W), const),
                vec(HEADS_W), pl.BlockSpec((LORA_W, HEADS_W), const),
                vec(HEADS_W), vec(HEADS_W), pl.BlockSpec((LANES, LANES), const)]
    n_halves = tt // (2 * CHUNK)
    packed_w = 2 * HEADS_W
    n_pc = N_PAIRS * 2 * n_halves
    y = pl.pallas_call(
        functools.partial(_scan_kernel, tt=tt, nt=nt, rev=rev),
        grid=(total + 2,),
        in_specs=in_specs,
        out_specs=pl.BlockSpec((None, tt, HEADS_W),
                               lambda s: (tile_of(jnp.maximum(s - 2, 0)), 0, 0)),
        out_shape=jax.ShapeDtypeStruct((total, tt, HEADS_W), F32),
        scratch_shapes=[pltpu.VMEM((N_PAIRS, LANES, LANES), F32),
                        pltpu.VMEM((2, n_halves, len(_PRE_BF16), CHUNK, packed_w), BF16),
                        pltpu.VMEM((2, n_halves, CHUNK + HALO_ROWS, packed_w), F32),
                        pltpu.VMEM((2, n_pc, LANES + CHUNK, LANES), BF16),
                        pltpu.VMEM((2, n_pc, LANES + CHUNK + HALO_ROWS, LANES), F32)],
        compiler_params=pltpu.CompilerParams(
            dimension_semantics=("arbitrary",), vmem_limit_bytes=VMEM_LIMIT),
        name="rwkv_scan_bwd" if rev else "rwkv_scan_fwd",
    )(zs_tiles, w0, wdec, a0, wa, k_k, k_a, seg)
    return y.reshape(b * t, HEADS_W)


def _mix_ffn_kernel(x_ref, att_ref, conv_ref, zs_ref, yf_ref, yb_ref,
                    a0_ref, wa_ref, wg_ref, ka_ref, rk_ref, lnw_ref, lnb_ref, seg_ref,
                    ga_ref, gb_ref, wout_ref, gmix_ref, gpre_ref, w1_ref, w2_ref, gffn_ref,
                    o_ref, *, ff_tile):
    seg = seg_ref[...]

    r = zs_ref[:, 0:HEADS_W]
    k = zs_ref[:, HEADS_W:2 * HEADS_W]
    v = zs_ref[:, 2 * HEADS_W:3 * HEADS_W]
    lora = zs_ref[:, 3 * HEADS_W:RWKV_IN]
    lora_b = lora.astype(BF16)
    asum = (_sigmoid(a0_ref[0:1, :] + _dot(lora_b, wa_ref[0]))
            + _sigmoid(a0_ref[1:2, :] + _dot(lora_b, wa_ref[1])))
    gate = _dot(_sigmoid(lora).astype(BF16), wg_ref[...])
    ksum = k * (2.0 + (asum - 2.0) * ka_ref[...])
    bonus = _segsum(r * ksum * rk_ref[...], seg) * v
    y = yf_ref[...] + yb_ref[...]
    yc = y - _segsum(y, seg) * (1.0 / HEAD_DIM)
    var = _segsum(yc * yc, seg) * (1.0 / HEAD_DIM)
    yn = yc * lax.rsqrt(var + LNX_EPS) * lnw_ref[...] + lnb_ref[...]
    y_rwkv = ((yn + bonus) * gate).astype(BF16)

    ya = _rms(att_ref[...], ga_ref[...]).astype(BF16)
    yb = _rms(conv_ref[...], gb_ref[...]).astype(BF16)
    mix = (_dot(ya, wout_ref[0:HEADS_W, :])
           + _dot(yb, wout_ref[HEADS_W:HEADS_W + CONV_W, :])
           + _dot(y_rwkv, wout_ref[HEADS_W + CONV_W:, :]))
    x = x_ref[...] + _rms(mix, gmix_ref[...])

    h = _rms(x, gpre_ref[...]).astype(BF16)
    acc = None
    for j in range(D_FF // ff_tile):
        sl = slice(j * ff_tile, (j + 1) * ff_tile)
        a = jnp.maximum(_dot(h, w1_ref[:, sl]), 0.0)
        part = _dot((a * a).astype(BF16), w2_ref[sl, :])
        acc = part if acc is None else acc + part
    o_ref[...] = x + _rms(acc, gffn_ref[...])


def _mix_ffn(x2, att, conv, zs, yf, yb, lp, tm):
    n = x2.shape[0]
    row = lambda i: (i, 0)
    tile = lambda w: pl.BlockSpec((tm, w), row)
    fixed = lambda *shape: pl.BlockSpec(shape, lambda i: (0,) * len(shape),
                                        pipeline_mode=pl.Buffered(1))
    return pl.pallas_call(
        functools.partial(_mix_ffn_kernel, ff_tile=1024),
        grid=(n // tm,),
        in_specs=[tile(D_MODEL), tile(HEADS_W), tile(CONV_W), tile(RWKV_IN), tile(HEADS_W),
                  tile(HEADS_W),
                  fixed(2, HEADS_W), fixed(2, LORA_W, HEADS_W), fixed(LORA_W, HEADS_W),
                  fixed(1, HEADS_W), fixed(1, HEADS_W), fixed(1, HEADS_W), fixed(1, HEADS_W),
                  fixed(LANES, LANES),
                  fixed(1, HEADS_W), fixed(1, CONV_W), fixed(D_MODEL, D_MODEL),
                  fixed(1, D_MODEL), fixed(1, D_MODEL), fixed(D_MODEL, D_FF),
                  fixed(D_FF, D_MODEL), fixed(1, D_MODEL)],
        out_specs=tile(D_MODEL),
        out_shape=jax.ShapeDtypeStruct((n, D_MODEL), F32),
        compiler_params=pltpu.CompilerParams(
            dimension_semantics=("parallel",), vmem_limit_bytes=VMEM_LIMIT),
        name="mix_ffn",
    )(x2, att, conv, zs, yf, yb,
      lp["a0"], lp["wa"], lp["wg"], lp["k_a"], lp["r_k"], lp["lnx_w"], lp["lnx_b"], lp["seg"],
      lp["attn_g"], lp["conv_g"], lp["w_out"], lp["g_mix_post"], lp["g_ffn_pre"],
      lp["w1"], lp["w2"], lp["g_ffn_post"])


def _pad_rows(w, start):
    return jnp.zeros((LORA_W, HEADS_W), F32).at[start:start + w.shape[0]].set(w)


def _layer_params(l, p):
    row = lambda a: a.reshape(1, -1).astype(F32)
    seg = np.kron(np.eye(LANES // HEAD_DIM, dtype=np.float32),
                  np.ones((HEAD_DIM, HEAD_DIM), np.float32))
    return dict(
        g_mix_pre=row(p["norm_mix_pre"][l]), g_mix_post=row(p["norm_mix_post"][l]),
        g_ffn_pre=row(p["norm_ffn_pre"][l]), g_ffn_post=row(p["norm_ffn_post"][l]),
        w_in=p["w_in"][l].astype(BF16), w_out=p["w_out"][l].astype(BF16),
        attn_g=row(p["attn_out_g"][l]), conv_w=p["conv_w"][l].astype(F32),
        conv_g=row(p["conv_out_g"][l]), mu=row(p["rwkv_mu"][l]),
        w0=p["decay_w0"][l].astype(F32), a0=p["iclr_a0"][l].astype(F32),
        wdec=jnp.stack([_pad_rows(p["decay_up"][l, d], 0) for d in range(2)]).astype(BF16),
        wa=jnp.stack([_pad_rows(p["iclr_up"][l, d], DECAY_RANK) for d in range(2)]).astype(BF16),
        wg=_pad_rows(p["gate_up"][l], DECAY_RANK + ICLR_RANK).astype(BF16),
        k_k=row(p["k_k"][l]), k_a=row(p["k_a"][l]), r_k=row(p["r_k"][l]),
        lnx_w=row(p["lnx_w"][l]), lnx_b=row(p["lnx_b"][l]),
        w1=p["ffn_w1"][l].astype(BF16), w2=p["ffn_w2"][l].astype(BF16),
        seg=jnp.asarray(seg, BF16),
    )


def _layer(x2, lp, biases, b, t):
    n = b * t
    tm = min(512, n)
    tt = min(SCAN_TILE, t)
    zq, conv, zs = _inproj(x2, lp["g_mix_pre"], lp["w_in"], lp["mu"], lp["conv_w"], tm, t)
    att = _attention(zq, biases, b, t)
    zs3 = zs.reshape(b, t, RWKV_IN)
    ys = [_scan(zs3, lp["w0"][d:d + 1], lp["wdec"][d], lp["a0"][d:d + 1], lp["wa"][d],
                lp["k_k"], lp["k_a"], lp["seg"], b, t, tt, rev=bool(d)).reshape(n, HEADS_W)
          for d in range(2)]
    return _mix_ffn(x2, att, conv, zs, ys[0], ys[1], lp, tm)


def _trunk(x, rel_bias, layers):
    b, t, _ = x.shape
    biases = [_attn_bias(rel_bias, t, dil) for dil in DILATIONS]
    x2 = x.reshape(b * t, D_MODEL)
    for lp in layers:
        x2 = _layer(x2, lp, biases, b, t)
    return x2.reshape(b, t, D_MODEL)


def kernel(x_prompt, x_sample, rel_bias, norm_mix_pre, norm_mix_post, norm_ffn_pre, norm_ffn_post, w_in, w_out, attn_out_g, conv_w, conv_out_g, rwkv_mu, decay_w0, decay_up, iclr_a0, iclr_up, gate_up, k_k, k_a, r_k, lnx_w, lnx_b, ffn_w1, ffn_w2):
    p = dict(norm_mix_pre=norm_mix_pre, norm_mix_post=norm_mix_post, norm_ffn_pre=norm_ffn_pre,
             norm_ffn_post=norm_ffn_post, w_in=w_in, w_out=w_out, attn_out_g=attn_out_g,
             conv_w=conv_w, conv_out_g=conv_out_g, rwkv_mu=rwkv_mu, decay_w0=decay_w0,
             decay_up=decay_up, iclr_a0=iclr_a0, iclr_up=iclr_up, gate_up=gate_up, k_k=k_k,
             k_a=k_a, r_k=r_k, lnx_w=lnx_w, lnx_b=lnx_b, ffn_w1=ffn_w1, ffn_w2=ffn_w2)
    layers = [_layer_params(l, p) for l in range(w_in.shape[0])]
    return (_trunk(x_prompt, rel_bias, layers), _trunk(x_sample, rel_bias, layers))
```

```python
import functools
import math

import numpy as np
import jax
import jax.numpy as jnp
from jax import lax
from jax.experimental import pallas as pl
from jax.experimental.pallas import tpu as pltpu

F32 = jnp.float32
BF16 = jnp.bfloat16

D_MODEL = 1024
HEAD_DIM = 64
N_HEADS = 6
HEADS_W = N_HEADS * HEAD_DIM
CONV_W = 256
QKV_W = 3 * HEADS_W
CONVIN_W = 3 * CONV_W
LORA_W = 128
DECAY_RANK = 32
ICLR_RANK = 32
RWKV_IN = 3 * HEADS_W + LORA_W
IN_WIDTH = QKV_W + CONVIN_W + RWKV_IN
D_FF = 4 * D_MODEL
DILATIONS = (1, 4, 16)
KEYS_PER_SIDE = 64
ATTN_UNROLL = 8
SCAN_TILE = 512
SCAN_CHAIN_EVERY = 2
SCAN_PREP_PER_STAGE = 1
_DONE = object()
N_BUCKETS = 32
BUCKET_MAX_DIST = 1024
LOG2E = math.log2(math.e)
RMS_EPS = 1e-6
LNX_EPS = 64e-5
CHUNK = 64
LANES = 128
N_PAIRS = HEADS_W // LANES
HALO_ROWS = 8
VMEM_LIMIT = 56 * 1024 * 1024


def _rms(x, g, eps=RMS_EPS):
    return x * lax.rsqrt(jnp.mean(x * x, axis=-1, keepdims=True) + eps) * g


def _sigmoid(x):
    return 1.0 / (1.0 + jnp.exp(-x))


def _dot(a, b):
    return jnp.dot(a, b, preferred_element_type=F32)


def _dot_tb(a, b):
    return lax.dot_general(a, b, (((1,), (1,)), ((), ())), preferred_element_type=F32)


def _dot_ta(a, b):
    return lax.dot_general(a, b, (((0,), (0,)), ((), ())), preferred_element_type=F32)


def _split2(x):
    hi = x.astype(BF16)
    lo = (x - hi.astype(F32)).astype(BF16)
    return hi, lo


def _segsum(x, pair_ones):
    xb = x.astype(BF16)
    return jnp.concatenate([_dot(xb[:, p * LANES:(p + 1) * LANES], pair_ones)
                            for p in range(x.shape[1] // LANES)], axis=1)


def _neighbours(main, prev_row, next_row):
    tt = main.shape[0]
    row = lax.broadcasted_iota(jnp.int32, main.shape, 0)
    prev = jnp.where(row == 0, prev_row, pltpu.roll(main, 1, 0))
    nxt = jnp.where(row == tt - 1, next_row, pltpu.roll(main, tt - 1, 0))
    return prev, nxt


def _inproj_kernel(x_ref, xp_ref, xn_ref, g_ref, w_ref, mu_ref, cw_ref,
                   zq_ref, conv_ref, zs_ref, *, tm, t):
    i = pl.program_id(0)
    x_ext = jnp.concatenate([x_ref[...], xp_ref[...], xn_ref[...]], axis=0)
    h_ext = _rms(x_ext, g_ref[...]).astype(BF16)
    zq_ref[...] = _dot(h_ext[:tm], w_ref[:, 0:QKV_W])
    z_ext = _dot(h_ext, w_ref[:, QKV_W:IN_WIDTH])
    prev_row = jnp.where((i * tm) % t == 0, 0.0,
                         z_ext[tm + HALO_ROWS - 1:tm + HALO_ROWS, :])
    next_row = jnp.where(((i + 1) * tm) % t == 0, 0.0,
                         z_ext[tm + HALO_ROWS:tm + HALO_ROWS + 1, :])
    z = z_ext[:tm]

    gated = lambda a: a[:, CONV_W:2 * CONV_W] * a[:, 2 * CONV_W:3 * CONV_W]
    u = gated(z)
    u_prev, u_next = _neighbours(u, gated(prev_row), gated(next_row))
    conv_ref[...] = z[:, 0:CONV_W] * (
        u_prev * cw_ref[0:1, :] + u * cw_ref[1:2, :] + u_next * cw_ref[2:3, :])

    zc = z[:, CONVIN_W:]
    c_prev, c_next = _neighbours(zc, prev_row[:, CONVIN_W:], next_row[:, CONVIN_W:])
    zs_ref[...] = zc + (0.5 * (c_prev + c_next) - zc) * mu_ref[...]


def _inproj(x2, g, w_bf16, mu, cw, tm, t):
    n = x2.shape[0]
    per = tm // HALO_ROWS
    last = n // HALO_ROWS - 1
    row = lambda i: (i, 0)
    const = lambda i: (0, 0)
    return pl.pallas_call(
        functools.partial(_inproj_kernel, tm=tm, t=t),
        grid=(n // tm,),
        in_specs=[pl.BlockSpec((tm, D_MODEL), row),
                  pl.BlockSpec((HALO_ROWS, D_MODEL), lambda i: (jnp.maximum(i * per - 1, 0), 0)),
                  pl.BlockSpec((HALO_ROWS, D_MODEL), lambda i: (jnp.minimum((i + 1) * per, last), 0)),
                  pl.BlockSpec((1, D_MODEL), const),
                  pl.BlockSpec((D_MODEL, IN_WIDTH), const),
                  pl.BlockSpec((1, RWKV_IN), const),
                  pl.BlockSpec((3, CONV_W), const)],
        out_specs=[pl.BlockSpec((tm, QKV_W), row),
                   pl.BlockSpec((tm, CONV_W), row),
                   pl.BlockSpec((tm, RWKV_IN), row)],
        out_shape=[jax.ShapeDtypeStruct((n, QKV_W), F32),
                   jax.ShapeDtypeStruct((n, CONV_W), F32),
                   jax.ShapeDtypeStruct((n, RWKV_IN), F32)],
        compiler_params=pltpu.CompilerParams(
            dimension_semantics=("parallel",), vmem_limit_bytes=VMEM_LIMIT),
        name="inproj",
    )(x2, x2, x2, g, w_bf16, mu, cw)


def _t5_bucket(rel):
    half = N_BUCKETS // 2
    max_exact = half // 2
    ret = np.where(rel > 0, half, 0)
    n = np.abs(rel)
    large = max_exact + (np.log(np.maximum(n, 1) / max_exact)
                         / np.log(BUCKET_MAX_DIST / max_exact) * (half - max_exact)).astype(np.int32)
    large = np.minimum(large, half - 1)
    return (ret + np.where(n < max_exact, n, large)).astype(np.int32)


def _attn_cfg(t, dil):
    cls_len = t // dil
    bq = min(128, cls_len)
    wk = min(bq + 2 * KEYS_PER_SIDE, cls_len)
    return cls_len, bq, wk, cls_len // bq


def _attn_bias(rel_bias, t, dil):
    _, bq, wk, _ = _attn_cfg(t, dil)
    out = []
    for shift in (0, KEYS_PER_SIDE, wk - bq):
        rel = np.arange(wk)[None, :] - np.arange(bq)[:, None] - shift
        valid = np.abs(rel) <= KEYS_PER_SIDE
        onehot = jnp.asarray(_t5_bucket(rel * dil))[:, :, None] == jnp.arange(N_BUCKETS)
        table = jnp.transpose(rel_bias.astype(F32))[:, None, None, :]
        bias = jnp.sum(jnp.where(onehot[None], table, 0.0), axis=-1)
        out.append(jnp.where(valid[None], bias * LOG2E, -jnp.inf))
    return jnp.stack(out)


def _attn_kernel(q_ref, k_ref, v_ref, b0_ref, b1_ref, b2_ref, o_ref, m_ref, s_ref, *, t):
    lane = lax.broadcasted_iota(jnp.int32, (1, LANES), 1)
    head0 = lane < HEAD_DIM
    bias_refs = (b0_ref, b1_ref, b2_ref)

    for di, dil in reversed(list(enumerate(DILATIONS))):
        cls_len, bq, wk, nb = _attn_cfg(t, dil)
        bias_ref = bias_refs[di]
        first_pass = dil == DILATIONS[-1]

        def rows(start, size, dil=dil):
            if dil == 1:
                return pl.ds(start, size)
            return pl.ds(start, size, stride=dil)

        def do_blocks(blocks, first_pass=first_pass, dil=dil, bq=bq, wk=wk, cls_len=cls_len,
                      bias_ref=bias_ref, rows=rows):
            qrows, q, kw, vw = [], [], [], []
            for c, blk, placement in blocks:
                m0 = blk * bq
                ws = (0, m0 - KEYS_PER_SIDE, cls_len - wk)[placement]
                qrows.append(rows(c + dil * m0, bq))
                krows = rows(c + dil * ws, wk)
                q.append(q_ref[qrows[-1], :] * (HEAD_DIM ** -0.5 * LOG2E))
                kw.append(k_ref[krows, :].astype(BF16))
                vw.append(v_ref[krows, :].astype(BF16))
            items = [(i, h) for i in range(len(blocks)) for h in range(2)]
            hmask = (head0, jnp.logical_not(head0))
            logits = {(i, h): _dot_tb(jnp.where(hmask[h], q[i], 0.0).astype(BF16), kw[i])
                      + bias_ref[blocks[i][2], h] for i, h in items}
            mh = {k: jnp.max(logits[k], axis=-1, keepdims=True) for k in items}
            p = {k: jnp.exp2(logits[k] - mh[k]) for k in items}
            sh = {k: jnp.sum(p[k], axis=-1, keepdims=True) for k in items}
            oh = {(i, h): _dot(p[(i, h)].astype(BF16), vw[i]) for i, h in items}
            for i in range(len(blocks)):
                m_blk = jnp.where(head0, mh[(i, 0)], mh[(i, 1)])
                s_blk = jnp.where(head0, sh[(i, 0)], sh[(i, 1)])
                o_blk = jnp.where(head0, oh[(i, 0)], oh[(i, 1)])
                if first_pass:
                    m_ref[qrows[i], :] = m_blk
                    s_ref[qrows[i], :] = s_blk
                    o_ref[qrows[i], :] = o_blk
                else:
                    m_old = m_ref[qrows[i], :]
                    m_new = jnp.maximum(m_old, m_blk)
                    a_old = jnp.exp2(m_old - m_new)
                    a_blk = jnp.exp2(m_blk - m_new)
                    m_ref[qrows[i], :] = m_new
                    s_ref[qrows[i], :] = s_ref[qrows[i], :] * a_old + s_blk * a_blk
                    o_ref[qrows[i], :] = o_ref[qrows[i], :] * a_old + o_blk * a_blk

        def loop(n, body):
            if n == 1:
                body(0)
            else:
                lax.fori_loop(0, n, lambda i, carry: (body(i), carry)[1], 0)

        edge = [(0, 0)] + ([(nb - 1, 2)] if nb > 1 else [])
        cb = math.gcd(dil, max(1, ATTN_UNROLL // len(edge)))
        loop(dil // cb, lambda i, edge=edge, cb=cb, do_blocks=do_blocks: do_blocks(
            [(i * cb + j, blk, placement) for j in range(cb) for blk, placement in edge]))
        n_mid = max(nb - 2, 0)
        if n_mid:
            u = max(d for d in range(1, ATTN_UNROLL + 1) if n_mid % d == 0)
            per = n_mid // u
            loop(dil * per, lambda i, u=u, per=per, do_blocks=do_blocks: do_blocks(
                [(i // per, 1 + (i % per) * u + j, 1) for j in range(u)]))

    rb = min(256, t)

    def norm_body(i, carry):
        sl = pl.ds(pl.multiple_of(i * rb, rb), rb)
        o_ref[sl, :] = o_ref[sl, :] / s_ref[sl, :]
        return carry

    lax.fori_loop(0, t // rb, norm_body, 0)


def _attention(zq, biases, b, t):
    zq3 = zq.reshape(b, t, QKV_W)
    in_specs = [pl.BlockSpec((None, t, LANES), lambda i, hp: (i, 0, hp)),
                pl.BlockSpec((None, t, LANES), lambda i, hp: (i, 0, N_PAIRS + hp)),
                pl.BlockSpec((None, t, LANES), lambda i, hp: (i, 0, 2 * N_PAIRS + hp))]
    for bias in biases:
        in_specs.append(pl.BlockSpec((3, 2) + bias.shape[2:], lambda i, hp: (0, hp, 0, 0)))
    out = pl.pallas_call(
        functools.partial(_attn_kernel, t=t),
        grid=(b, N_PAIRS),
        in_specs=in_specs,
        out_specs=pl.BlockSpec((None, t, LANES), lambda i, hp: (i, 0, hp)),
        out_shape=jax.ShapeDtypeStruct((b, t, HEADS_W), F32),
        scratch_shapes=[pltpu.VMEM((t, LANES), F32), pltpu.VMEM((t, LANES), F32)],
        compiler_params=pltpu.CompilerParams(
            dimension_semantics=("parallel", "parallel"), vmem_limit_bytes=VMEM_LIMIT),
        name="dilated_attention",
    )(zq3, zq3, zq3, *biases)
    return out.reshape(b * t, HEADS_W)


_PRE_BF16 = ("r_t", "a_t", "b_t", "k_t", "b_e", "k_e", "v")


def _scan_kernel(zs_ref, w0_ref, wdec_ref, a0_ref, wa_ref, kk_ref, ka_ref, seg_ref,
                 y_ref, state_ref, pre_b_ref, pre_f_ref, res_b_ref, res_f_ref,
                 *, tt, nt, rev):
    step = pl.program_id(0)
    slot_w = step % 2
    slot_r = 1 - slot_w
    n_halves = tt // (2 * CHUNK)

    @pl.when(step == 0)
    def _():
        pre_b_ref[1] = jnp.zeros(pre_b_ref.shape[1:], BF16)
        pre_f_ref[1] = jnp.zeros(pre_f_ref.shape[1:], F32)
        res_b_ref[0] = jnp.zeros(res_b_ref.shape[1:], BF16)
        res_f_ref[0] = jnp.zeros(res_f_ref.shape[1:], F32)
        state_ref[...] = jnp.zeros_like(state_ref)

    trow = lax.broadcasted_iota(jnp.int32, (CHUNK, CHUNK), 0)
    tcol = lax.broadcasted_iota(jnp.int32, (CHUNK, CHUNK), 1)
    tri = jnp.where((tcol >= trow) if rev else (tcol <= trow), 1.0, 0.0).astype(BF16)

    def prep_task(g):
        rows = slice(g * 2 * CHUNK, (g + 1) * 2 * CHUNK)
        lora = zs_ref[rows, 3 * HEADS_W:RWKV_IN]
        xdec = w0_ref[...] + _dot(jnp.tanh(lora).astype(BF16), wdec_ref[...])
        lw = (-math.exp(-0.5)) * _sigmoid(xdec)
        yield
        asig = _sigmoid(a0_ref[...] + _dot(lora.astype(BF16), wa_ref[...]))
        yield
        k = zs_ref[rows, HEADS_W:2 * HEADS_W]
        kk = k * kk_ref[...]
        ssq = _segsum(kk * kk, seg_ref[...])
        yield
        kk = kk / jnp.maximum(jnp.sqrt(ssq), 1e-12)
        kdir = k * (1.0 + (asig - 1.0) * ka_ref[...])
        bvec = kk * asig
        yield
        pack = lambda x: jnp.concatenate([x[:CHUNK], x[CHUNK:]], axis=1)
        lw_p = pack(lw)
        lw_hi, lw_lo = _split2(lw_p)
        cum = _dot(tri, lw_hi) + _dot(tri, lw_lo)
        yield
        last = 0 if rev else CHUNK - 1
        tot = cum[last:last + 1, :]
        e_neg = jnp.exp(-cum)
        e_end = jnp.exp(tot - cum)
        yield
        r_t = pack(zs_ref[rows, 0:HEADS_W]) * jnp.exp(cum)
        pre_f_ref[slot_w, g, 0:CHUNK, :] = r_t
        pre_f_ref[slot_w, g, CHUNK:CHUNK + 1, :] = jnp.exp(tot)
        yield
        kd_p, bv_p = pack(kdir), pack(bvec)
        out = dict(r_t=r_t, a_t=-pack(kk) * jnp.exp(cum - lw_p), b_t=bv_p * e_neg,
                   k_t=kd_p * e_neg, b_e=bv_p * e_end, k_e=kd_p * e_end,
                   v=pack(zs_ref[rows, 2 * HEADS_W:3 * HEADS_W]))
        for i, name in enumerate(_PRE_BF16):
            pre_b_ref[slot_w, g, i] = out[name].astype(BF16)
            yield

    def load_pre(g):
        pre = {name: pre_b_ref[slot_r, g, i] for i, name in enumerate(_PRE_BF16)}
        pre["r_f"] = pre_f_ref[slot_r, g, 0:CHUNK, :]
        pre["p_end"] = pre_f_ref[slot_r, g, CHUNK:CHUNK + 1, :]
        return pre

    gw = 2 * LANES
    head0 = lax.broadcasted_iota(jnp.int32, (1, LANES), 1) < HEAD_DIM
    tpos = lax.broadcasted_iota(jnp.int32, (CHUNK, gw), 0)
    spos = lax.broadcasted_iota(jnp.int32, (CHUNK, gw), 1) % CHUNK
    if rev:
        incl, strict = spos >= tpos, spos > tpos
    else:
        incl, strict = spos <= tpos, spos < tpos
    eye_g = jnp.where(spos == tpos, 1.0, 0.0)
    prow = lax.broadcasted_iota(jnp.int32, (LANES, LANES), 0)
    pcol = lax.broadcasted_iota(jnp.int32, (LANES, LANES), 1)
    pair_diag = (prow // HEAD_DIM) == (pcol // HEAD_DIM)
    pair_eye = prow == pcol

    def bd(xb):
        lo, hi = xb[:, :LANES], xb[:, LANES:]
        z = jnp.zeros_like(lo)
        return jnp.concatenate([
            jnp.concatenate([jnp.where(head0, lo, z), z], axis=1),
            jnp.concatenate([jnp.where(head0, z, lo), z], axis=1),
            jnp.concatenate([z, jnp.where(head0, hi, z)], axis=1),
            jnp.concatenate([z, jnp.where(head0, z, hi)], axis=1)], axis=0)

    def piece(x, p, cc):
        lane0 = cc * HEADS_W + p * LANES
        return x[:, lane0:lane0 + LANES]

    def grp(x, p):
        return jnp.concatenate([piece(x, p, 0), piece(x, p, 1)], axis=1)

    def sibling(m):
        return jnp.logical_and(tpos // (2 * m) == spos // (2 * m), tpos // m != spos // m)

    pairs = range(N_PAIRS)
    cc_order = (1, 0) if rev else (0, 1)

    def group_task(pre, p, g):
        rg = grp(pre["r_f"], p)
        ag_b = grp(pre["a_t"], p)
        lhs = jnp.concatenate([ag_b, grp(pre["r_t"], p)], axis=0)
        s1 = _dot_tb(lhs, bd(grp(pre["b_t"], p)))
        yield
        s2 = _dot_tb(lhs, bd(grp(pre["k_t"], p)))
        yield
        a_ab = jnp.where(strict, s1[:CHUNK], 0.0)
        a_rb = jnp.where(incl, s1[CHUNK:], 0.0).astype(BF16)
        a_ak = jnp.where(strict, s2[:CHUNK], 0.0).astype(BF16)
        a_rk = jnp.where(incl, s2[CHUNK:], 0.0).astype(BF16)
        tinv = eye_g + jnp.where(sibling(1), a_ab, 0.0)
        m = 2
        while m < CHUNK:
            a_off = jnp.where(sibling(m), a_ab, 0.0).astype(BF16)
            db = tinv.astype(BF16)
            yb = _dot(a_off, bd(db)).astype(BF16)
            yield
            tinv = tinv + _dot(db, bd(yb))
            yield
            m *= 2
        tb = tinv.astype(BF16)
        vbd = bd(grp(pre["v"], p))
        w2 = _dot(a_ak, vbd).astype(BF16)
        yield
        ta = _dot(tb, bd(ag_b))
        yield
        tv = _dot(tb, bd(w2))
        yield
        qg = rg + _dot(a_rb, bd(ta.astype(BF16)))
        yield
        yg = _dot(a_rb, bd(tv.astype(BF16))) + _dot(a_rk, vbd)
        yield
        for cc in cc_order:
            half = slice(cc * LANES, (cc + 1) * LANES)
            be = piece(pre["b_e"], p, cc)
            x_kk = jnp.where(pair_diag, _dot_ta(be, ta[:, half].astype(BF16)), 0.0)
            yield
            h_kv = jnp.where(pair_diag, _dot_ta(
                jnp.concatenate([be, piece(pre["k_e"], p, cc)], axis=0),
                jnp.concatenate([tv[:, half].astype(BF16), piece(pre["v"], p, cc)], axis=0)), 0.0)
            yield
            p_col = jnp.sum(jnp.where(pair_eye, piece(pre["p_end"], p, cc), 0.0),
                            axis=1, keepdims=True)
            i = p * (2 * n_halves) + 2 * g + cc
            res_b_ref[slot_r, i, 0:CHUNK, :] = qg[:, half].astype(BF16)
            res_b_ref[slot_r, i, CHUNK:CHUNK + LANES, :] = x_kk.astype(BF16)
            res_f_ref[slot_r, i, 0:LANES, :] = h_kv
            res_f_ref[slot_r, i, LANES:2 * LANES, :] = jnp.broadcast_to(p_col, (LANES, LANES))
            res_f_ref[slot_r, i, 2 * LANES:2 * LANES + CHUNK, :] = yg[:, half]

    fresh = (step - 2) % nt == 0
    state = {p: jnp.where(fresh, 0.0, state_ref[p]) for p in pairs}

    def chain_task():
        for g in halves:
            for cc in cc_order:
                c = 2 * g + cc
                for p in pairs:
                    i = p * (2 * n_halves) + c
                    prod = _dot(res_b_ref[slot_w, i], state[p].astype(BF16))
                    y_ref[c * CHUNK:(c + 1) * CHUNK, p * LANES:(p + 1) * LANES] = (
                        res_f_ref[slot_w, i, 2 * LANES:2 * LANES + CHUNK, :] + prod[:CHUNK])
                    state[p] = (state[p] * res_f_ref[slot_w, i, LANES:2 * LANES, :]
                                + prod[CHUNK:] + res_f_ref[slot_w, i, 0:LANES, :])
                yield

    halves = list(range(n_halves))
    if rev:
        halves.reverse()
    tasks = [group_task(load_pre(g), p, g) for g in halves for p in pairs]
    preps = [prep_task(g) for g in halves]
    chain = chain_task()
    rounds = 0
    while tasks:
        tasks = [task for task in tasks if next(task, _DONE) is not _DONE]
        for _ in range(SCAN_PREP_PER_STAGE):
            if preps and next(preps[0], _DONE) is _DONE:
                preps.pop(0)
        if rounds % SCAN_CHAIN_EVERY == SCAN_CHAIN_EVERY - 1:
            next(chain, _DONE)
        rounds += 1
    for task in preps + [chain]:
        for _ in task:
            pass
    for p in pairs:
        state_ref[p] = state[p]


def _scan(zs3, w0, wdec, a0, wa, k_k, k_a, seg, b, t, tt, rev):
    nt = t // tt
    total = b * nt
    zs_tiles = zs3.reshape(total, tt, RWKV_IN)

    def tile_of(pos):
        if not rev:
            return pos
        return (pos // nt) * nt + (nt - 1 - pos % nt)

    const = lambda s: (0, 0)
    vec = lambda w: pl.BlockSpec((1, w), const)
    in_specs = [pl.BlockSpec((None, tt, RWKV_IN),
                             lambda s: (tile_of(jnp.minimum(s, total - 1)), 0, 0)),
                vec(HEADS_W), pl.BlockSpec((LORA_W, HEADS_W), const),
                vec(HEADS_W), pl.BlockSpec((LORA_W, HEADS_W), const),
                vec(HEADS_W), vec(HEADS_W), pl.BlockSpec((LANES, LANES), const)]
    n_halves = tt // (2 * CHUNK)
    packed_w = 2 * HEADS_W
    n_pc = N_PAIRS * 2 * n_halves
    y = pl.pallas_call(
        functools.partial(_scan_kernel, tt=tt, nt=nt, rev=rev),
        grid=(total + 2,),
        in_specs=in_specs,
        out_specs=pl.BlockSpec((None, tt, HEADS_W),
                               lambda s: (tile_of(jnp.maximum(s - 2, 0)), 0, 0)),
        out_shape=jax.ShapeDtypeStruct((total, tt, HEADS_W), F32),
        scratch_shapes=[pltpu.VMEM((N_PAIRS, LANES, LANES), F32),
                        pltpu.VMEM((2, n_halves, len(_PRE_BF16), CHUNK, packed_w), BF16),
                        pltpu.VMEM((2, n_halves, CHUNK + HALO_ROWS, packed_w), F32),
                        pltpu.VMEM((2, n_pc, LANES + CHUNK, LANES), BF16),
                        pltpu.VMEM((2, n_pc, 2 * LANES + CHUNK, LANES), F32)],
        compiler_params=pltpu.CompilerParams(
            dimension_semantics=("arbitrary",), vmem_limit_bytes=VMEM_LIMIT),
        name="rwkv_scan_bwd" if rev else "rwkv_scan_fwd",
    )(zs_tiles, w0, wdec, a0, wa, k_k, k_a, seg)
    return y.reshape(b * t, HEADS_W)


def _mix_ffn_kernel(x_ref, att_ref, conv_ref, zs_ref, yf_ref, yb_ref,
                    a0_ref, wa_ref, wg_ref, ka_ref, rk_ref, lnw_ref, lnb_ref, seg_ref,
                    ga_ref, gb_ref, wout_ref, gmix_ref, gpre_ref, w1_ref, w2_ref, gffn_ref,
                    o_ref, *, ff_tile):
    seg = seg_ref[...]

    r = zs_ref[:, 0:HEADS_W]
    k = zs_ref[:, HEADS_W:2 * HEADS_W]
    v = zs_ref[:, 2 * HEADS_W:3 * HEADS_W]
    lora = zs_ref[:, 3 * HEADS_W:RWKV_IN]
    lora_b = lora.astype(BF16)
    asum = (_sigmoid(a0_ref[0:1, :] + _dot(lora_b, wa_ref[0]))
            + _sigmoid(a0_ref[1:2, :] + _dot(lora_b, wa_ref[1])))
    gate = _dot(_sigmoid(lora).astype(BF16), wg_ref[...])
    ksum = k * (2.0 + (asum - 2.0) * ka_ref[...])
    bonus = _segsum(r * ksum * rk_ref[...], seg) * v
    y = yf_ref[...] + yb_ref[...]
    yc = y - _segsum(y, seg) * (1.0 / HEAD_DIM)
    var = _segsum(yc * yc, seg) * (1.0 / HEAD_DIM)
    yn = yc * lax.rsqrt(var + LNX_EPS) * lnw_ref[...] + lnb_ref[...]
    y_rwkv = ((yn + bonus) * gate).astype(BF16)

    ya = _rms(att_ref[...], ga_ref[...]).astype(BF16)
    yb = _rms(conv_ref[...], gb_ref[...]).astype(BF16)
    mix = (_dot(ya, wout_ref[0:HEADS_W, :])
           + _dot(yb, wout_ref[HEADS_W:HEADS_W + CONV_W, :])
           + _dot(y_rwkv, wout_ref[HEADS_W + CONV_W:, :]))
    x = x_ref[...] + _rms(mix, gmix_ref[...])

    h = _rms(x, gpre_ref[...]).astype(BF16)
    acc = None
    for j in range(D_FF // ff_tile):
        sl = slice(j * ff_tile, (j + 1) * ff_tile)
        a = jnp.maximum(_dot(h, w1_ref[:, sl]), 0.0)
        part = _dot((a * a).astype(BF16), w2_ref[sl, :])
        acc = part if acc is None else acc + part
    o_ref[...] = x + _rms(acc, gffn_ref[...])


def _mix_ffn(x2, att, conv, zs, yf, yb, lp, tm):
    n = x2.shape[0]
    row = lambda i: (i, 0)
    tile = lambda w: pl.BlockSpec((tm, w), row)
    fixed = lambda *shape: pl.BlockSpec(shape, lambda i: (0,) * len(shape),
                                        pipeline_mode=pl.Buffered(1))
    return pl.pallas_call(
        functools.partial(_mix_ffn_kernel, ff_tile=1024),
        grid=(n // tm,),
        in_specs=[tile(D_MODEL), tile(HEADS_W), tile(CONV_W), tile(RWKV_IN), tile(HEADS_W),
                  tile(HEADS_W),
                  fixed(2, HEADS_W), fixed(2, LORA_W, HEADS_W), fixed(LORA_W, HEADS_W),
                  fixed(1, HEADS_W), fixed(1, HEADS_W), fixed(1, HEADS_W), fixed(1, HEADS_W),
                  fixed(LANES, LANES),
                  fixed(1, HEADS_W), fixed(1, CONV_W), fixed(D_MODEL, D_MODEL),
                  fixed(1, D_MODEL), fixed(1, D_MODEL), fixed(D_MODEL, D_FF),
                  fixed(D_FF, D_MODEL), fixed(1, D_MODEL)],
        out_specs=tile(D_MODEL),
        out_shape=jax.ShapeDtypeStruct((n, D_MODEL), F32),
        compiler_params=pltpu.CompilerParams(
            dimension_semantics=("parallel",), vmem_limit_bytes=VMEM_LIMIT),
        name="mix_ffn",
    )(x2, att, conv, zs, yf, yb,
      lp["a0"], lp["wa"], lp["wg"], lp["k_a"], lp["r_k"], lp["lnx_w"], lp["lnx_b"], lp["seg"],
      lp["attn_g"], lp["conv_g"], lp["w_out"], lp["g_mix_post"], lp["g_ffn_pre"],
      lp["w1"], lp["w2"], lp["g_ffn_post"])


def _pad_rows(w, start):
    return jnp.zeros((LORA_W, HEADS_W), F32).at[start:start + w.shape[0]].set(w)


def _layer_params(l, p):
    row = lambda a: a.reshape(1, -1).astype(F32)
    seg = np.kron(np.eye(LANES // HEAD_DIM, dtype=np.float32),
                  np.ones((HEAD_DIM, HEAD_DIM), np.float32))
    return dict(
        g_mix_pre=row(p["norm_mix_pre"][l]), g_mix_post=row(p["norm_mix_post"][l]),
        g_ffn_pre=row(p["norm_ffn_pre"][l]), g_ffn_post=row(p["norm_ffn_post"][l]),
        w_in=p["w_in"][l].astype(BF16), w_out=p["w_out"][l].astype(BF16),
        attn_g=row(p["attn_out_g"][l]), conv_w=p["conv_w"][l].astype(F32),
        conv_g=row(p["conv_out_g"][l]), mu=row(p["rwkv_mu"][l]),
        w0=p["decay_w0"][l].astype(F32), a0=p["iclr_a0"][l].astype(F32),
        wdec=jnp.stack([_pad_rows(p["decay_up"][l, d], 0) for d in range(2)]).astype(BF16),
        wa=jnp.stack([_pad_rows(p["iclr_up"][l, d], DECAY_RANK) for d in range(2)]).astype(BF16),
        wg=_pad_rows(p["gate_up"][l], DECAY_RANK + ICLR_RANK).astype(BF16),
        k_k=row(p["k_k"][l]), k_a=row(p["k_a"][l]), r_k=row(p["r_k"][l]),
        lnx_w=row(p["lnx_w"][l]), lnx_b=row(p["lnx_b"][l]),
        w1=p["ffn_w1"][l].astype(BF16), w2=p["ffn_w2"][l].astype(BF16),
        seg=jnp.asarray(seg, BF16),
    )


def _layer(x2, lp, biases, b, t):
    n = b * t
    tm = min(512, n)
    tt = min(SCAN_TILE, t)
    zq, conv, zs = _inproj(x2, lp["g_mix_pre"], lp["w_in"], lp["mu"], lp["conv_w"], tm, t)
    att = _attention(zq, biases, b, t)
    zs3 = zs.reshape(b, t, RWKV_IN)
    ys = [_scan(zs3, lp["w0"][d:d + 1], lp["wdec"][d], lp["a0"][d:d + 1], lp["wa"][d],
                lp["k_k"], lp["k_a"], lp["seg"], b, t, tt, rev=bool(d)).reshape(n, HEADS_W)
          for d in range(2)]
    return _mix_ffn(x2, att, conv, zs, ys[0], ys[1], lp, tm)


def _trunk(x, rel_bias, layers):
    b, t, _ = x.shape
    biases = [_attn_bias(rel_bias, t, dil) for dil in DILATIONS]
    x2 = x.reshape(b * t, D_MODEL)
    for lp in layers:
        x2 = _layer(x2, lp, biases, b, t)
    return x2.reshape(b, t, D_MODEL)


def kernel(x_prompt, x_sample, rel_bias, norm_mix_pre, norm_mix_post, norm_ffn_pre, norm_ffn_post, w_in, w_out, attn_out_g, conv_w, conv_out_g, rwkv_mu, decay_w0, decay_up, iclr_a0, iclr_up, gate_up, k_k, k_a, r_k, lnx_w, lnx_b, ffn_w1, ffn_w2):
    p = dict(norm_mix_pre=norm_mix_pre, norm_mix_post=norm_mix_post, norm_ffn_pre=norm_ffn_pre,
             norm_ffn_post=norm_ffn_post, w_in=w_in, w_out=w_out, attn_out_g=attn_out_g,
             conv_w=conv_w, conv_out_g=conv_out_g, rwkv_mu=rwkv_mu, decay_w0=decay_w0,
             decay_up=decay_up, iclr_a0=iclr_a0, iclr_up=iclr_up, gate_up=gate_up, k_k=k_k,
             k_a=k_a, r_k=r_k, lnx_w=lnx_w, lnx_b=lnx_b, ffn_w1=ffn_w1, ffn_w2=ffn_w2)
    layers = [_layer_params(l, p) for l in range(w_in.shape[0])]
    return (_trunk(x_prompt, rel_bias, layers), _trunk(x_sample, rel_bias, layers))
```

```python
import functools
import math

import numpy as np
import jax
import jax.numpy as jnp
from jax import lax
from jax.experimental import pallas as pl
from jax.experimental.pallas import tpu as pltpu

F32 = jnp.float32
BF16 = jnp.bfloat16

D_MODEL = 1024
HEAD_DIM = 64
N_HEADS = 6
HEADS_W = N_HEADS * HEAD_DIM
CONV_W = 256
QKV_W = 3 * HEADS_W
CONVIN_W = 3 * CONV_W
LORA_W = 128
DECAY_RANK = 32
ICLR_RANK = 32
RWKV_IN = 3 * HEADS_W + LORA_W
IN_WIDTH = QKV_W + CONVIN_W + RWKV_IN
D_FF = 4 * D_MODEL
DILATIONS = (1, 4, 16)
KEYS_PER_SIDE = 64
ATTN_UNROLL = 8
SCAN_TILE = 512
SCAN_CHAIN_EVERY = 2
SCAN_PREP_PER_STAGE = 1
_DONE = object()
N_BUCKETS = 32
BUCKET_MAX_DIST = 1024
LOG2E = math.log2(math.e)
RMS_EPS = 1e-6
LNX_EPS = 64e-5
CHUNK = 64
LANES = 128
N_PAIRS = HEADS_W // LANES
HALO_ROWS = 8
VMEM_LIMIT = 56 * 1024 * 1024


def _rms(x, g, eps=RMS_EPS):
    return x * lax.rsqrt(jnp.mean(x * x, axis=-1, keepdims=True) + eps) * g


def _sigmoid(x):
    return 1.0 / (1.0 + jnp.exp(-x))


def _dot(a, b):
    return jnp.dot(a, b, preferred_element_type=F32)


def _dot_tb(a, b):
    return lax.dot_general(a, b, (((1,), (1,)), ((), ())), preferred_element_type=F32)


def _dot_ta(a, b):
    return lax.dot_general(a, b, (((0,), (0,)), ((), ())), preferred_element_type=F32)


def _split2(x):
    hi = x.astype(BF16)
    lo = (x - hi.astype(F32)).astype(BF16)
    return hi, lo


def _segsum(x, pair_ones):
    xb = x.astype(BF16)
    return jnp.concatenate([_dot(xb[:, p * LANES:(p + 1) * LANES], pair_ones)
                            for p in range(x.shape[1] // LANES)], axis=1)


def _neighbours(main, prev_row, next_row):
    tt = main.shape[0]
    row = lax.broadcasted_iota(jnp.int32, main.shape, 0)
    prev = jnp.where(row == 0, prev_row, pltpu.roll(main, 1, 0))
    nxt = jnp.where(row == tt - 1, next_row, pltpu.roll(main, tt - 1, 0))
    return prev, nxt


def _inproj_kernel(x_ref, xp_ref, xn_ref, g_ref, w_ref, mu_ref, cw_ref,
                   zq_ref, conv_ref, zs_ref, *, tm, t):
    i = pl.program_id(0)
    x_ext = jnp.concatenate([x_ref[...], xp_ref[...], xn_ref[...]], axis=0)
    h_ext = _rms(x_ext, g_ref[...]).astype(BF16)
    zq_ref[...] = _dot(h_ext[:tm], w_ref[:, 0:QKV_W])
    z_ext = _dot(h_ext, w_ref[:, QKV_W:IN_WIDTH])
    prev_row = jnp.where((i * tm) % t == 0, 0.0,
                         z_ext[tm + HALO_ROWS - 1:tm + HALO_ROWS, :])
    next_row = jnp.where(((i + 1) * tm) % t == 0, 0.0,
                         z_ext[tm + HALO_ROWS:tm + HALO_ROWS + 1, :])
    z = z_ext[:tm]

    gated = lambda a: a[:, CONV_W:2 * CONV_W] * a[:, 2 * CONV_W:3 * CONV_W]
    u = gated(z)
    u_prev, u_next = _neighbours(u, gated(prev_row), gated(next_row))
    conv_ref[...] = z[:, 0:CONV_W] * (
        u_prev * cw_ref[0:1, :] + u * cw_ref[1:2, :] + u_next * cw_ref[2:3, :])

    zc = z[:, CONVIN_W:]
    c_prev, c_next = _neighbours(zc, prev_row[:, CONVIN_W:], next_row[:, CONVIN_W:])
    zs_ref[...] = zc + (0.5 * (c_prev + c_next) - zc) * mu_ref[...]


def _inproj(x2, g, w_bf16, mu, cw, tm, t):
    n = x2.shape[0]
    per = tm // HALO_ROWS
    last = n // HALO_ROWS - 1
    row = lambda i: (i, 0)
    const = lambda i: (0, 0)
    return pl.pallas_call(
        functools.partial(_inproj_kernel, tm=tm, t=t),
        grid=(n // tm,),
        in_specs=[pl.BlockSpec((tm, D_MODEL), row),
                  pl.BlockSpec((HALO_ROWS, D_MODEL), lambda i: (jnp.maximum(i * per - 1, 0), 0)),
                  pl.BlockSpec((HALO_ROWS, D_MODEL), lambda i: (jnp.minimum((i + 1) * per, last), 0)),
                  pl.BlockSpec((1, D_MODEL), const),
                  pl.BlockSpec((D_MODEL, IN_WIDTH), const),
                  pl.BlockSpec((1, RWKV_IN), const),
                  pl.BlockSpec((3, CONV_W), const)],
        out_specs=[pl.BlockSpec((tm, QKV_W), row),
                   pl.BlockSpec((tm, CONV_W), row),
                   pl.BlockSpec((tm, RWKV_IN), row)],
        out_shape=[jax.ShapeDtypeStruct((n, QKV_W), F32),
                   jax.ShapeDtypeStruct((n, CONV_W), F32),
                   jax.ShapeDtypeStruct((n, RWKV_IN), F32)],
        compiler_params=pltpu.CompilerParams(
            dimension_semantics=("parallel",), vmem_limit_bytes=VMEM_LIMIT),
        name="inproj",
    )(x2, x2, x2, g, w_bf16, mu, cw)


def _t5_bucket(rel):
    half = N_BUCKETS // 2
    max_exact = half // 2
    ret = np.where(rel > 0, half, 0)
    n = np.abs(rel)
    large = max_exact + (np.log(np.maximum(n, 1) / max_exact)
                         / np.log(BUCKET_MAX_DIST / max_exact) * (half - max_exact)).astype(np.int32)
    large = np.minimum(large, half - 1)
    return (ret + np.where(n < max_exact, n, large)).astype(np.int32)


def _attn_cfg(t, dil):
    cls_len = t // dil
    bq = min(128, cls_len)
    wk = min(bq + 2 * KEYS_PER_SIDE, cls_len)
    return cls_len, bq, wk, cls_len // bq


def _attn_bias(rel_bias, t, dil):
    _, bq, wk, _ = _attn_cfg(t, dil)
    out = []
    for shift in (0, KEYS_PER_SIDE, wk - bq):
        rel = np.arange(wk)[None, :] - np.arange(bq)[:, None] - shift
        valid = np.abs(rel) <= KEYS_PER_SIDE
        onehot = jnp.asarray(_t5_bucket(rel * dil))[:, :, None] == jnp.arange(N_BUCKETS)
        table = jnp.transpose(rel_bias.astype(F32))[:, None, None, :]
        bias = jnp.sum(jnp.where(onehot[None], table, 0.0), axis=-1)
        out.append(jnp.where(valid[None], bias * LOG2E, -jnp.inf))
    return jnp.stack(out)


def _attn_kernel(q_ref, k_ref, v_ref, b0_ref, b1_ref, b2_ref, o_ref, m_ref, s_ref, *, t):
    lane = lax.broadcasted_iota(jnp.int32, (1, LANES), 1)
    head0 = lane < HEAD_DIM
    bias_refs = (b0_ref, b1_ref, b2_ref)

    for di, dil in reversed(list(enumerate(DILATIONS))):
        cls_len, bq, wk, nb = _attn_cfg(t, dil)
        bias_ref = bias_refs[di]
        first_pass = dil == DILATIONS[-1]

        def rows(start, size, dil=dil):
            if dil == 1:
                return pl.ds(start, size)
            return pl.ds(start, size, stride=dil)

        def do_blocks(blocks, first_pass=first_pass, dil=dil, bq=bq, wk=wk, cls_len=cls_len,
                      bias_ref=bias_ref, rows=rows):
            qrows, q, kw, vw = [], [], [], []
            for c, blk, placement in blocks:
                m0 = blk * bq
                ws = (0, m0 - KEYS_PER_SIDE, cls_len - wk)[placement]
                qrows.append(rows(c + dil * m0, bq))
                krows = rows(c + dil * ws, wk)
                q.append(q_ref[qrows[-1], :] * (HEAD_DIM ** -0.5 * LOG2E))
                kw.append(k_ref[krows, :].astype(BF16))
                vw.append(v_ref[krows, :].astype(BF16))
            items = [(i, h) for i in range(len(blocks)) for h in range(2)]
            hmask = (head0, jnp.logical_not(head0))
            logits = {(i, h): _dot_tb(jnp.where(hmask[h], q[i], 0.0).astype(BF16), kw[i])
                      + bias_ref[blocks[i][2], h] for i, h in items}
            mh = {k: jnp.max(logits[k], axis=-1, keepdims=True) for k in items}
            p = {k: jnp.exp2(logits[k] - mh[k]) for k in items}
            sh = {k: jnp.sum(p[k], axis=-1, keepdims=True) for k in items}
            oh = {(i, h): _dot(p[(i, h)].astype(BF16), vw[i]) for i, h in items}
            for i in range(len(blocks)):
                m_blk = jnp.where(head0, mh[(i, 0)], mh[(i, 1)])
                s_blk = jnp.where(head0, sh[(i, 0)], sh[(i, 1)])
                o_blk = jnp.where(head0, oh[(i, 0)], oh[(i, 1)])
                if first_pass:
                    m_ref[qrows[i], :] = m_blk
                    s_ref[qrows[i], :] = s_blk
                    o_ref[qrows[i], :] = o_blk
                else:
                    m_old = m_ref[qrows[i], :]
                    m_new = jnp.maximum(m_old, m_blk)
                    a_old = jnp.exp2(m_old - m_new)
                    a_blk = jnp.exp2(m_blk - m_new)
                    m_ref[qrows[i], :] = m_new
                    s_ref[qrows[i], :] = s_ref[qrows[i], :] * a_old + s_blk * a_blk
                    o_ref[qrows[i], :] = o_ref[qrows[i], :] * a_old + o_blk * a_blk

        def loop(n, body):
            if n == 1:
                body(0)
            else:
                lax.fori_loop(0, n, lambda i, carry: (body(i), carry)[1], 0)

        edge = [(0, 0)] + ([(nb - 1, 2)] if nb > 1 else [])
        cb = math.gcd(dil, max(1, ATTN_UNROLL // len(edge)))
        loop(dil // cb, lambda i, edge=edge, cb=cb, do_blocks=do_blocks: do_blocks(
            [(i * cb + j, blk, placement) for j in range(cb) for blk, placement in edge]))
        n_mid = max(nb - 2, 0)
        if n_mid:
            u = max(d for d in range(1, ATTN_UNROLL + 1) if n_mid % d == 0)
            per = n_mid // u
            loop(dil * per, lambda i, u=u, per=per, do_blocks=do_blocks: do_blocks(
                [(i // per, 1 + (i % per) * u + j, 1) for j in range(u)]))

    rb = min(256, t)

    def norm_body(i, carry):
        sl = pl.ds(pl.multiple_of(i * rb, rb), rb)
        o_ref[sl, :] = o_ref[sl, :] / s_ref[sl, :]
        return carry

    lax.fori_loop(0, t // rb, norm_body, 0)


def _attention(zq, biases, b, t):
    zq3 = zq.reshape(b, t, QKV_W)
    in_specs = [pl.BlockSpec((None, t, LANES), lambda i, hp: (i, 0, hp)),
                pl.BlockSpec((None, t, LANES), lambda i, hp: (i, 0, N_PAIRS + hp)),
                pl.BlockSpec((None, t, LANES), lambda i, hp: (i, 0, 2 * N_PAIRS + hp))]
    for bias in biases:
        in_specs.append(pl.BlockSpec((3, 2) + bias.shape[2:], lambda i, hp: (0, hp, 0, 0)))
    out = pl.pallas_call(
        functools.partial(_attn_kernel, t=t),
        grid=(b, N_PAIRS),
        in_specs=in_specs,
        out_specs=pl.BlockSpec((None, t, LANES), lambda i, hp: (i, 0, hp)),
        out_shape=jax.ShapeDtypeStruct((b, t, HEADS_W), F32),
        scratch_shapes=[pltpu.VMEM((t, LANES), F32), pltpu.VMEM((t, LANES), F32)],
        compiler_params=pltpu.CompilerParams(
            dimension_semantics=("parallel", "parallel"), vmem_limit_bytes=VMEM_LIMIT),
        name="dilated_attention",
    )(zq3, zq3, zq3, *biases)
    return out.reshape(b * t, HEADS_W)


_PRE_BF16 = ("r_t", "a_t", "b_t", "k_t", "b_e", "k_e", "v")


def _scan_kernel(zs_ref, w0_ref, wdec_ref, a0_ref, wa_ref, kk_ref, ka_ref, seg_ref,
                 y_ref, state_ref, pre_b_ref, pre_f_ref, res_b_ref, res_f_ref,
                 *, tt, nt, rev):
    step = pl.program_id(0)
    slot_w = step % 2
    slot_r = 1 - slot_w
    n_halves = tt // (2 * CHUNK)

    @pl.when(step == 0)
    def _():
        pre_b_ref[1] = jnp.zeros(pre_b_ref.shape[1:], BF16)
        pre_f_ref[1] = jnp.zeros(pre_f_ref.shape[1:], F32)
        res_b_ref[0] = jnp.zeros(res_b_ref.shape[1:], BF16)
        res_f_ref[0] = jnp.zeros(res_f_ref.shape[1:], F32)
        state_ref[...] = jnp.zeros_like(state_ref)

    trow = lax.broadcasted_iota(jnp.int32, (CHUNK, CHUNK), 0)
    tcol = lax.broadcasted_iota(jnp.int32, (CHUNK, CHUNK), 1)
    tri = jnp.where((tcol >= trow) if rev else (tcol <= trow), 1.0, 0.0).astype(BF16)

    def prep_task(g):
        rows = slice(g * 2 * CHUNK, (g + 1) * 2 * CHUNK)
        lora = zs_ref[rows, 3 * HEADS_W:RWKV_IN]
        xdec = w0_ref[...] + _dot(jnp.tanh(lora).astype(BF16), wdec_ref[...])
        lw = (-math.exp(-0.5)) * _sigmoid(xdec)
        yield
        asig = _sigmoid(a0_ref[...] + _dot(lora.astype(BF16), wa_ref[...]))
        yield
        k = zs_ref[rows, HEADS_W:2 * HEADS_W]
        kk = k * kk_ref[...]
        ssq = _segsum(kk * kk, seg_ref[...])
        yield
        kk = kk / jnp.maximum(jnp.sqrt(ssq), 1e-12)
        kdir = k * (1.0 + (asig - 1.0) * ka_ref[...])
        bvec = kk * asig
        yield
        pack = lambda x: jnp.concatenate([x[:CHUNK], x[CHUNK:]], axis=1)
        lw_p = pack(lw)
        lw_hi, lw_lo = _split2(lw_p)
        cum = _dot(tri, lw_hi) + _dot(tri, lw_lo)
        yield
        last = 0 if rev else CHUNK - 1
        tot = cum[last:last + 1, :]
        e_neg = jnp.exp(-cum)
        e_end = jnp.exp(tot - cum)
        yield
        r_t = pack(zs_ref[rows, 0:HEADS_W]) * jnp.exp(cum)
        pre_f_ref[slot_w, g, 0:CHUNK, :] = r_t
        pre_f_ref[slot_w, g, CHUNK:CHUNK + 1, :] = jnp.exp(tot)
        yield
        kd_p, bv_p = pack(kdir), pack(bvec)
        out = dict(r_t=r_t, a_t=-pack(kk) * jnp.exp(cum - lw_p), b_t=bv_p * e_neg,
                   k_t=kd_p * e_neg, b_e=bv_p * e_end, k_e=kd_p * e_end,
                   v=pack(zs_ref[rows, 2 * HEADS_W:3 * HEADS_W]))
        for i, name in enumerate(_PRE_BF16):
            pre_b_ref[slot_w, g, i] = out[name].astype(BF16)
            yield

    def load_pre(g):
        pre = {name: pre_b_ref[slot_r, g, i] for i, name in enumerate(_PRE_BF16)}
        pre["r_f"] = pre_f_ref[slot_r, g, 0:CHUNK, :]
        pre["p_end"] = pre_f_ref[slot_r, g, CHUNK:CHUNK + 1, :]
        return pre

    gw = 2 * LANES
    head0 = lax.broadcasted_iota(jnp.int32, (1, LANES), 1) < HEAD_DIM
    tpos = lax.broadcasted_iota(jnp.int32, (CHUNK, gw), 0)
    spos = lax.broadcasted_iota(jnp.int32, (CHUNK, gw), 1) % CHUNK
    if rev:
        incl, strict = spos >= tpos, spos > tpos
    else:
        incl, strict = spos <= tpos, spos < tpos
    eye_g = jnp.where(spos == tpos, 1.0, 0.0)
    prow = lax.broadcasted_iota(jnp.int32, (LANES, LANES), 0)
    pcol = lax.broadcasted_iota(jnp.int32, (LANES, LANES), 1)
    pair_diag = (prow // HEAD_DIM) == (pcol // HEAD_DIM)
    pair_eye = prow == pcol

    def bd(xb):
        lo, hi = xb[:, :LANES], xb[:, LANES:]
        z = jnp.zeros_like(lo)
        return jnp.concatenate([
            jnp.concatenate([jnp.where(head0, lo, z), z], axis=1),
            jnp.concatenate([jnp.where(head0, z, lo), z], axis=1),
            jnp.concatenate([z, jnp.where(head0, hi, z)], axis=1),
            jnp.concatenate([z, jnp.where(head0, z, hi)], axis=1)], axis=0)

    def piece(x, p, cc):
        lane0 = cc * HEADS_W + p * LANES
        return x[:, lane0:lane0 + LANES]

    def grp(x, p):
        return jnp.concatenate([piece(x, p, 0), piece(x, p, 1)], axis=1)

    def sibling(m):
        return jnp.logical_and(tpos // (2 * m) == spos // (2 * m), tpos // m != spos // m)

    pairs = range(N_PAIRS)
    cc_order = (1, 0) if rev else (0, 1)

    def group_task(pre, p, g):
        rg = grp(pre["r_f"], p)
        ag_b = grp(pre["a_t"], p)
        lhs = jnp.concatenate([ag_b, grp(pre["r_t"], p)], axis=0)
        s1 = _dot_tb(lhs, bd(grp(pre["b_t"], p)))
        yield
        s2 = _dot_tb(lhs, bd(grp(pre["k_t"], p)))
        yield
        a_ab = jnp.where(strict, s1[:CHUNK], 0.0)
        a_rb = jnp.where(incl, s1[CHUNK:], 0.0).astype(BF16)
        a_ak = jnp.where(strict, s2[:CHUNK], 0.0).astype(BF16)
        a_rk = jnp.where(incl, s2[CHUNK:], 0.0).astype(BF16)
        tinv = eye_g + jnp.where(sibling(1), a_ab, 0.0)
        m = 2
        while m < CHUNK:
            a_off = jnp.where(sibling(m), a_ab, 0.0).astype(BF16)
            db = tinv.astype(BF16)
            yb = _dot(a_off, bd(db)).astype(BF16)
            yield
            tinv = tinv + _dot(db, bd(yb))
            yield
            m *= 2
        tb = tinv.astype(BF16)
        av = _dot(jnp.concatenate([a_ak, a_rk], axis=0), bd(grp(pre["v"], p)))
        w2 = av[:CHUNK].astype(BF16)
        yield
        ta = _dot(tb, bd(ag_b))
        yield
        tv = _dot(tb, bd(w2))
        yield
        qg = rg + _dot(a_rb, bd(ta.astype(BF16)))
        yield
        yg = _dot(a_rb, bd(tv.astype(BF16))) + av[CHUNK:]
        yield
        for cc in cc_order:
            half = slice(cc * LANES, (cc + 1) * LANES)
            be = piece(pre["b_e"], p, cc)
            top = jnp.concatenate([ta[:, half], tv[:, half]], axis=1).astype(BF16)
            bot = jnp.concatenate([jnp.zeros((CHUNK, LANES), BF16), piece(pre["v"], p, cc)],
                                  axis=1)
            xh = _dot_ta(jnp.concatenate([be, piece(pre["k_e"], p, cc)], axis=0),
                         jnp.concatenate([top, bot], axis=0))
            x_kk = jnp.where(pair_diag, xh[:, :LANES], 0.0)
            h_kv = jnp.where(pair_diag, xh[:, LANES:], 0.0)
            yield
            p_col = jnp.sum(jnp.where(pair_eye, piece(pre["p_end"], p, cc), 0.0),
                            axis=1, keepdims=True)
            i = p * (2 * n_halves) + 2 * g + cc
            res_b_ref[slot_r, i, 0:CHUNK, :] = qg[:, half].astype(BF16)
            res_b_ref[slot_r, i, CHUNK:CHUNK + LANES, :] = x_kk.astype(BF16)
            res_f_ref[slot_r, i, 0:LANES, :] = h_kv
            res_f_ref[slot_r, i, LANES:2 * LANES, :] = jnp.broadcast_to(p_col, (LANES, LANES))
            res_f_ref[slot_r, i, 2 * LANES:2 * LANES + CHUNK, :] = yg[:, half]

    fresh = (step - 2) % nt == 0
    state = {p: jnp.where(fresh, 0.0, state_ref[p]) for p in pairs}

    def chain_task():
        for g in halves:
            for cc in cc_order:
                c = 2 * g + cc
                for p in pairs:
                    i = p * (2 * n_halves) + c
                    prod = _dot(res_b_ref[slot_w, i], state[p].astype(BF16))
                    y_ref[c * CHUNK:(c + 1) * CHUNK, p * LANES:(p + 1) * LANES] = (
                        res_f_ref[slot_w, i, 2 * LANES:2 * LANES + CHUNK, :] + prod[:CHUNK])
                    state[p] = (state[p] * res_f_ref[slot_w, i, LANES:2 * LANES, :]
                                + prod[CHUNK:] + res_f_ref[slot_w, i, 0:LANES, :])
                yield

    halves = list(range(n_halves))
    if rev:
        halves.reverse()
    tasks = [group_task(load_pre(g), p, g) for g in halves for p in pairs]
    preps = [prep_task(g) for g in halves]
    chain = chain_task()
    rounds = 0
    while tasks:
        tasks = [task for task in tasks if next(task, _DONE) is not _DONE]
        for _ in range(SCAN_PREP_PER_STAGE):
            if preps and next(preps[0], _DONE) is _DONE:
                preps.pop(0)
        if rounds % SCAN_CHAIN_EVERY == SCAN_CHAIN_EVERY - 1:
            next(chain, _DONE)
        rounds += 1
    for task in preps + [chain]:
        for _ in task:
            pass
    for p in pairs:
        state_ref[p] = state[p]


def _scan(zs3, w0, wdec, a0, wa, k_k, k_a, seg, b, t, tt, rev):
    nt = t // tt
    total = b * nt
    zs_tiles = zs3.reshape(total, tt, RWKV_IN)

    def tile_of(pos):
        if not rev:
            return pos
        return (pos // nt) * nt + (nt - 1 - pos % nt)

    const = lambda s: (0, 0)
    vec = lambda w: pl.BlockSpec((1, w), const)
    in_specs = [pl.BlockSpec((None, tt, RWKV_IN),
                             lambda s: (tile_of(jnp.minimum(s, total - 1)), 0, 0)),
                vec(HEADS_W), pl.BlockSpec((LORA_W, HEADS_W), const),
                vec(HEADS_W), pl.BlockSpec((LORA_W, HEADS_W), const),
                vec(HEADS_W), vec(HEADS_W), pl.BlockSpec((LANES, LANES), const)]
    n_halves = tt // (2 * CHUNK)
    packed_w = 2 * HEADS_W
    n_pc = N_PAIRS * 2 * n_halves
    y = pl.pallas_call(
        functools.partial(_scan_kernel, tt=tt, nt=nt, rev=rev),
        grid=(total + 2,),
        in_specs=in_specs,
        out_specs=pl.BlockSpec((None, tt, HEADS_W),
                               lambda s: (tile_of(jnp.maximum(s - 2, 0)), 0, 0)),
        out_shape=jax.ShapeDtypeStruct((total, tt, HEADS_W), F32),
        scratch_shapes=[pltpu.VMEM((N_PAIRS, LANES, LANES), F32),
                        pltpu.VMEM((2, n_halves, len(_PRE_BF16), CHUNK, packed_w), BF16),
                        pltpu.VMEM((2, n_halves, CHUNK + HALO_ROWS, packed_w), F32),
                        pltpu.VMEM((2, n_pc, LANES + CHUNK, LANES), BF16),
                        pltpu.VMEM((2, n_pc, 2 * LANES + CHUNK, LANES), F32)],
        compiler_params=pltpu.CompilerParams(
            dimension_semantics=("arbitrary",), vmem_limit_bytes=VMEM_LIMIT),
        name="rwkv_scan_bwd" if rev else "rwkv_scan_fwd",
    )(zs_tiles, w0, wdec, a0, wa, k_k, k_a, seg)
    return y.reshape(b * t, HEADS_W)


def _mix_ffn_kernel(x_ref, att_ref, conv_ref, zs_ref, yf_ref, yb_ref,
                    a0_ref, wa_ref, wg_ref, ka_ref, rk_ref, lnw_ref, lnb_ref, seg_ref,
                    ga_ref, gb_ref, wout_ref, gmix_ref, gpre_ref, w1_ref, w2_ref, gffn_ref,
                    o_ref, *, ff_tile):
    seg = seg_ref[...]

    r = zs_ref[:, 0:HEADS_W]
    k = zs_ref[:, HEADS_W:2 * HEADS_W]
    v = zs_ref[:, 2 * HEADS_W:3 * HEADS_W]
    lora = zs_ref[:, 3 * HEADS_W:RWKV_IN]
    lora_b = lora.astype(BF16)
    asum = (_sigmoid(a0_ref[0:1, :] + _dot(lora_b, wa_ref[0]))
            + _sigmoid(a0_ref[1:2, :] + _dot(lora_b, wa_ref[1])))
    gate = _dot(_sigmoid(lora).astype(BF16), wg_ref[...])
    ksum = k * (2.0 + (asum - 2.0) * ka_ref[...])
    bonus = _segsum(r * ksum * rk_ref[...], seg) * v
    y = yf_ref[...] + yb_ref[...]
    yc = y - _segsum(y, seg) * (1.0 / HEAD_DIM)
    var = _segsum(yc * yc, seg) * (1.0 / HEAD_DIM)
    yn = yc * lax.rsqrt(var + LNX_EPS) * lnw_ref[...] + lnb_ref[...]
    y_rwkv = ((yn + bonus) * gate).astype(BF16)

    ya = _rms(att_ref[...], ga_ref[...]).astype(BF16)
    yb = _rms(conv_ref[...], gb_ref[...]).astype(BF16)
    mix = (_dot(ya, wout_ref[0:HEADS_W, :])
           + _dot(yb, wout_ref[HEADS_W:HEADS_W + CONV_W, :])
           + _dot(y_rwkv, wout_ref[HEADS_W + CONV_W:, :]))
    x = x_ref[...] + _rms(mix, gmix_ref[...])

    h = _rms(x, gpre_ref[...]).astype(BF16)
    acc = None
    for j in range(D_FF // ff_tile):
        sl = slice(j * ff_tile, (j + 1) * ff_tile)
        a = jnp.maximum(_dot(h, w1_ref[:, sl]), 0.0)
        part = _dot((a * a).astype(BF16), w2_ref[sl, :])
        acc = part if acc is None else acc + part
    o_ref[...] = x + _rms(acc, gffn_ref[...])


def _mix_ffn(x2, att, conv, zs, yf, yb, lp, tm):
    n = x2.shape[0]
    row = lambda i: (i, 0)
    tile = lambda w: pl.BlockSpec((tm, w), row)
    fixed = lambda *shape: pl.BlockSpec(shape, lambda i: (0,) * len(shape),
                                        pipeline_mode=pl.Buffered(1))
    return pl.pallas_call(
        functools.partial(_mix_ffn_kernel, ff_tile=1024),
        grid=(n // tm,),
        in_specs=[tile(D_MODEL), tile(HEADS_W), tile(CONV_W), tile(RWKV_IN), tile(HEADS_W),
                  tile(HEADS_W),
                  fixed(2, HEADS_W), fixed(2, LORA_W, HEADS_W), fixed(LORA_W, HEADS_W),
                  fixed(1, HEADS_W), fixed(1, HEADS_W), fixed(1, HEADS_W), fixed(1, HEADS_W),
                  fixed(LANES, LANES),
                  fixed(1, HEADS_W), fixed(1, CONV_W), fixed(D_MODEL, D_MODEL),
                  fixed(1, D_MODEL), fixed(1, D_MODEL), fixed(D_MODEL, D_FF),
                  fixed(D_FF, D_MODEL), fixed(1, D_MODEL)],
        out_specs=tile(D_MODEL),
        out_shape=jax.ShapeDtypeStruct((n, D_MODEL), F32),
        compiler_params=pltpu.CompilerParams(
            dimension_semantics=("parallel",), vmem_limit_bytes=VMEM_LIMIT),
        name="mix_ffn",
    )(x2, att, conv, zs, yf, yb,
      lp["a0"], lp["wa"], lp["wg"], lp["k_a"], lp["r_k"], lp["lnx_w"], lp["lnx_b"], lp["seg"],
      lp["attn_g"], lp["conv_g"], lp["w_out"], lp["g_mix_post"], lp["g_ffn_pre"],
      lp["w1"], lp["w2"], lp["g_ffn_post"])


def _pad_rows(w, start):
    return jnp.zeros((LORA_W, HEADS_W), F32).at[start:start + w.shape[0]].set(w)


def _layer_params(l, p):
    row = lambda a: a.reshape(1, -1).astype(F32)
    seg = np.kron(np.eye(LANES // HEAD_DIM, dtype=np.float32),
                  np.ones((HEAD_DIM, HEAD_DIM), np.float32))
    return dict(
        g_mix_pre=row(p["norm_mix_pre"][l]), g_mix_post=row(p["norm_mix_post"][l]),
        g_ffn_pre=row(p["norm_ffn_pre"][l]), g_ffn_post=row(p["norm_ffn_post"][l]),
        w_in=p["w_in"][l].astype(BF16), w_out=p["w_out"][l].astype(BF16),
        attn_g=row(p["attn_out_g"][l]), conv_w=p["conv_w"][l].astype(F32),
        conv_g=row(p["conv_out_g"][l]), mu=row(p["rwkv_mu"][l]),
        w0=p["decay_w0"][l].astype(F32), a0=p["iclr_a0"][l].astype(F32),
        wdec=jnp.stack([_pad_rows(p["decay_up"][l, d], 0) for d in range(2)]).astype(BF16),
        wa=jnp.stack([_pad_rows(p["iclr_up"][l, d], DECAY_RANK) for d in range(2)]).astype(BF16),
        wg=_pad_rows(p["gate_up"][l], DECAY_RANK + ICLR_RANK).astype(BF16),
        k_k=row(p["k_k"][l]), k_a=row(p["k_a"][l]), r_k=row(p["r_k"][l]),
        lnx_w=row(p["lnx_w"][l]), lnx_b=row(p["lnx_b"][l]),
        w1=p["ffn_w1"][l].astype(BF16), w2=p["ffn_w2"][l].astype(BF16),
        seg=jnp.asarray(seg, BF16),
    )


def _layer(x2, lp, biases, b, t):
    n = b * t
    tm = min(512, n)
    tt = min(SCAN_TILE, t)
    zq, conv, zs = _inproj(x2, lp["g_mix_pre"], lp["w_in"], lp["mu"], lp["conv_w"], tm, t)
    att = _attention(zq, biases, b, t)
    zs3 = zs.reshape(b, t, RWKV_IN)
    ys = [_scan(zs3, lp["w0"][d:d + 1], lp["wdec"][d], lp["a0"][d:d + 1], lp["wa"][d],
                lp["k_k"], lp["k_a"], lp["seg"], b, t, tt, rev=bool(d)).reshape(n, HEADS_W)
          for d in range(2)]
    return _mix_ffn(x2, att, conv, zs, ys[0], ys[1], lp, tm)


def _trunk(x, rel_bias, layers):
    b, t, _ = x.shape
    biases = [_attn_bias(rel_bias, t, dil) for dil in DILATIONS]
    x2 = x.reshape(b * t, D_MODEL)
    for lp in layers:
        x2 = _layer(x2, lp, biases, b, t)
    return x2.reshape(b, t, D_MODEL)


def kernel(x_prompt, x_sample, rel_bias, norm_mix_pre, norm_mix_post, norm_ffn_pre, norm_ffn_post, w_in, w_out, attn_out_g, conv_w, conv_out_g, rwkv_mu, decay_w0, decay_up, iclr_a0, iclr_up, gate_up, k_k, k_a, r_k, lnx_w, lnx_b, ffn_w1, ffn_w2):
    p = dict(norm_mix_pre=norm_mix_pre, norm_mix_post=norm_mix_post, norm_ffn_pre=norm_ffn_pre,
             norm_ffn_post=norm_ffn_post, w_in=w_in, w_out=w_out, attn_out_g=attn_out_g,
             conv_w=conv_w, conv_out_g=conv_out_g, rwkv_mu=rwkv_mu, decay_w0=decay_w0,
             decay_up=decay_up, iclr_a0=iclr_a0, iclr_up=iclr_up, gate_up=gate_up, k_k=k_k,
             k_a=k_a, r_k=r_k, lnx_w=lnx_w, lnx_b=lnx_b, ffn_w1=ffn_w1, ffn_w2=ffn_w2)
    layers = [_layer_params(l, p) for l in range(w_in.shape[0])]
    return (_trunk(x_prompt, rel_bias, layers), _trunk(x_sample, rel_bias, layers))
```
